```python
import jax, jax.numpy as jnp
from jax import lax
import numpy as np

D_MODEL = 1024
BATCH = 8
SEQ = 4096
DEPTH = 1

CHUNK = 64
MIX_WIDTH = D_MODEL
POOL_WIDTH = MIX_WIDTH // 2
POOL_WINDOWS = (2, 4, 8, 16)
POOL_GROUPS = len(POOL_WINDOWS)
POOL_GROUP_DIM = POOL_WIDTH // POOL_GROUPS
HGRN_WIDTH = MIX_WIDTH - POOL_WIDTH
HGRN_EXPAND = 128
HGRN_HEADS = HGRN_WIDTH // HGRN_EXPAND
HGRN_DK = HGRN_EXPAND
HGRN_DV = HGRN_WIDTH // HGRN_HEADS
IN_COLS = POOL_WIDTH + 4 * HGRN_WIDTH
D_FF = 2816
CONV_WIDTH = 3
LN_EPS = 1e-5
RMS_EPS = 1e-6
ALPHA = (2.0 * DEPTH) ** 0.25
BETA = (8.0 * DEPTH) ** -0.25

kernel_name = "hymba_pool_hgrn2_deepnorm_block"


def layer_norm(x, g, b):
    xf = x.astype(jnp.float32)
    mu = jnp.mean(xf, axis=-1, keepdims=True)
    var = jnp.mean(jnp.square(xf - mu), axis=-1, keepdims=True)
    y = (xf - mu) * lax.rsqrt(var + LN_EPS)
    return (y * g + b).astype(x.dtype)


def pool_mixer(xp, pool_w, pool_scale):
    B, T, _ = xp.shape
    xf = xp.astype(jnp.float32)
    cs0 = jnp.pad(jnp.cumsum(xf, axis=1), ((0, 0), (1, 0), (0, 0)))
    pos = jnp.arange(1, T + 1, dtype=jnp.float32)
    outs = []
    for gi, w in enumerate(POOL_WINDOWS):
        sl = slice(gi * POOL_GROUP_DIM, (gi + 1) * POOL_GROUP_DIM)
        c = cs0[..., sl]
        lagged = jnp.pad(c, ((0, 0), (w - 1, 0), (0, 0)))[:, :T]
        mean = (c[:, 1:] - lagged) / jnp.minimum(pos, float(w))[None, :, None]
        d = (mean - xf[..., sl]).astype(xp.dtype)
        outs.append(jnp.einsum('btc,cd->btd', d, pool_w[gi]))
    return jnp.concatenate(outs, axis=-1) * pool_scale


def hgrn2_mixer(q, f_pre, i, g, lb, gnorm_w):
    B, T, _ = q.shape
    nC = T // CHUNK
    H = HGRN_HEADS
    lbf = lb.astype(jnp.float32)
    forget = lbf + (1.0 - lbf) * jax.nn.sigmoid(f_pre.astype(jnp.float32))
    log_f = jnp.log(forget)
    k = 1.0 - forget
    qf = jax.nn.silu(q.astype(jnp.float32))

    def to_chunks(t, d):
        return t.astype(jnp.float32).reshape(B, nC, CHUNK, H, d).transpose(1, 0, 3, 2, 4)

    qh, lfh, kh = to_chunks(qf, HGRN_DK), to_chunks(log_f, HGRN_DK), to_chunks(k, HGRN_DK)
    vh = to_chunks(i, HGRN_DV)
    causal = jnp.tril(jnp.ones((CHUNK, CHUNK), dtype=bool))

    def step(S, inp):
        qc, lfc, kc, vc = inp
        b = jnp.cumsum(lfc, axis=2)
        diff = b[:, :, :, None, :] - b[:, :, None, :, :]
        decay = jnp.exp(jnp.where(causal[:, :, None], diff, -jnp.inf))
        scores = jnp.einsum('bhtd,bhsd,bhtsd->bhts', qc, kc, decay)
        o = jnp.einsum('bhts,bhse->bhte', scores, vc) \
            + jnp.einsum('bhtd,bhde->bhte', qc * jnp.exp(b), S)
        b_last = b[:, :, -1:, :]
        S = jnp.exp(b_last)[:, :, 0, :, None] * S \
            + jnp.einsum('bhsd,bhse->bhde', kc * jnp.exp(b_last - b), vc)
        return S, o

    S0 = jnp.zeros((B, H, HGRN_DK, HGRN_DV), jnp.float32)
    _, o = lax.scan(step, S0, (qh, lfh, kh, vh))
    o = o.transpose(1, 0, 3, 2, 4).reshape(B, T, H, HGRN_DV)
    o = o * lax.rsqrt(jnp.mean(jnp.square(o), axis=-1, keepdims=True) + RMS_EPS) * gnorm_w
    o = o * jax.nn.silu(g.astype(jnp.float32).reshape(B, T, H, HGRN_DV))
    return o.reshape(B, T, HGRN_WIDTH).astype(q.dtype)


def conv_gated_mlp(h, w_up, conv_w, conv_b, w_down):
    u = jnp.einsum('btd,df->btf', h, w_up)
    T = u.shape[1]
    up = jnp.pad(u, ((0, 0), (CONV_WIDTH - 1, 0), (0, 0)))
    uc = conv_b + sum(conv_w[j] * up[:, j:j + T] for j in range(CONV_WIDTH))
    a, v = jnp.split(uc, 2, axis=-1)
    return jnp.einsum('btf,fd->btd', jax.nn.gelu(a) * v, w_down)


def setup_inputs(seed: int = 0) -> dict:
    key = jax.random.key(seed)
    ks = jax.random.split(key, 16)
    n = jax.random.normal
    f32 = jnp.float32
    return {
        "x": n(ks[0], (BATCH, SEQ, D_MODEL), f32),
        "w_in": n(ks[1], (DEPTH, D_MODEL, IN_COLS), f32) * D_MODEL ** -0.5,
        "pool_w": n(ks[2], (DEPTH, POOL_GROUPS, POOL_GROUP_DIM, POOL_GROUP_DIM), f32) * POOL_GROUP_DIM ** -0.5,
        "pool_scale": 1.0 + 0.02 * n(ks[3], (DEPTH, POOL_WIDTH), f32),
        "hgrn_lb": n(ks[4], (DEPTH + 1, HGRN_WIDTH), f32),
        "hgrn_gnorm": 1.0 + 0.02 * n(ks[5], (DEPTH, HGRN_DV), f32),
        "w_out": n(ks[6], (DEPTH, MIX_WIDTH, D_MODEL), f32) * (MIX_WIDTH ** -0.5 * BETA),
        "ln1_g": 1.0 + 0.02 * n(ks[7], (DEPTH, D_MODEL), f32),
        "ln1_b": 0.02 * n(ks[8], (DEPTH, D_MODEL), f32),
        "w_up": n(ks[9], (DEPTH, D_MODEL, 2 * D_FF), f32) * D_MODEL ** -0.5,
        "conv_w": n(ks[10], (DEPTH, CONV_WIDTH, 2 * D_FF), f32) * CONV_WIDTH ** -0.5,
        "conv_b": 0.02 * n(ks[11], (DEPTH, 2 * D_FF), f32),
        "w_down": n(ks[12], (DEPTH, D_FF, D_MODEL), f32) * (D_FF ** -0.5 * BETA),
        "ln2_g": 1.0 + 0.02 * n(ks[13], (DEPTH, D_MODEL), f32),
        "ln2_b": 0.02 * n(ks[14], (DEPTH, D_MODEL), f32),
    }


def reference(x, w_in, pool_w, pool_scale, hgrn_lb, hgrn_gnorm, w_out, ln1_g, ln1_b,
              w_up, conv_w, conv_b, w_down, ln2_g, ln2_b):
    lb_all = jnp.cumsum(jax.nn.softmax(hgrn_lb.astype(jnp.float32), axis=0), axis=0)
    splits = [POOL_WIDTH, POOL_WIDTH + HGRN_WIDTH, POOL_WIDTH + 2 * HGRN_WIDTH,
              POOL_WIDTH + 3 * HGRN_WIDTH]
    h = x
    for l in range(DEPTH):
        proj = jnp.einsum('btd,dc->btc', h, w_in[l])
        xp, q, f_pre, i, g = jnp.split(proj, splits, axis=-1)
        y_a = pool_mixer(xp, pool_w[l], pool_scale[l])
        y_b = hgrn2_mixer(q, f_pre, i, g, lb_all[l], hgrn_gnorm[l])
        mix = jnp.einsum('btc,cd->btd', jnp.concatenate([y_a, y_b], axis=-1), w_out[l])
        h = layer_norm(ALPHA * h + mix, ln1_g[l], ln1_b[l])
        ffn = conv_gated_mlp(h, w_up[l], conv_w[l], conv_b[l], w_down[l])
        h = layer_norm(ALPHA * h + ffn, ln2_g[l], ln2_b[l])
    return h
```

```python
import functools

import jax
import jax.numpy as jnp
from jax import lax
from jax.experimental import pallas as pl
from jax.experimental.pallas import tpu as pltpu

F32 = jnp.float32
BF16 = jnp.bfloat16

CHUNK = 64
POOL_WINDOWS = (2, 4, 8, 16)
GROUP = 128
POOL_HALO = 16
CONV_WIDTH = 3
CONV_HALO = 8
FF_CHUNK = 256
LN_EPS = 1e-5
RMS_EPS = 1e-6
LEVELS = (32, 16, 8, 4, 2, 1)

NT_DIMS = (((1,), (1,)), ((), ()))
TN_DIMS = (((0,), (0,)), ((), ()))


def _layer_norm(z, g, b):
    mu = jnp.mean(z, axis=-1, keepdims=True)
    zc = z - mu
    var = jnp.mean(zc * zc, axis=-1, keepdims=True)
    return zc * lax.rsqrt(var + LN_EPS) * g + b


def _split3_bf16(a):
    hi = a.astype(BF16)
    r1 = a - hi.astype(F32)
    mid = r1.astype(BF16)
    lo = (r1 - mid.astype(F32)).astype(BF16)
    return hi, mid, lo


def _group_ref_rows(b, half):
    n, c = b.shape
    g = 2 * half
    b3 = b.reshape(n // g, g, c)
    ref = jnp.broadcast_to(b3[:, half - 1:half, :], b3.shape)
    return ref.reshape(n, c)


def _level_scale(half, b, fgt, row):
    n = b.shape[0]
    if half == 1:
        return jnp.where((row & 1) == 1, fgt, 1.0)
    if half == 2:
        prev = pltpu.roll(fgt, 1, 0)
        nxt = pltpu.roll(fgt, n - 1, 0)
        j = row & 3
        return jnp.where(j == 0, nxt,
                         jnp.where(j == 1, 1.0,
                                   jnp.where(j == 2, fgt, fgt * prev)))
    ref = _group_ref_rows(b, half)
    later = (row & half) != 0
    return jnp.exp(jnp.where(later, b - ref, ref - b))


def _mixer_kernel(x_ref, w_in_ref, pool_w_ref, pool_scale_ref, lb_ref, gnorm_ref,
                  w_out_ref, g1_ref, b1_ref, out_ref, st_ref, halo_ref,
                  *, tm, alpha, heads):
    ti = pl.program_id(1)
    pw = GROUP * len(POOL_WINDOWS)
    hw = GROUP * heads

    @pl.when(ti == 0)
    def _():
        st_ref[...] = jnp.zeros_like(st_ref)
        halo_ref[...] = jnp.zeros_like(halo_ref)

    x = x_ref[0]
    proj = jnp.dot(x.astype(BF16), w_in_ref[...], preferred_element_type=F32)
    xp = proj[:, :pw]
    q = proj[:, pw:pw + hw]
    f_pre = proj[:, pw + hw:pw + 2 * hw]
    v = proj[:, pw + 2 * hw:pw + 3 * hw]
    g = proj[:, pw + 3 * hw:pw + 4 * hw]

    ext = jnp.concatenate([halo_ref[...], xp], axis=0)
    halo_ref[...] = xp[tm - POOL_HALO:, :]
    frames = (lax.broadcasted_iota(jnp.int32, (tm, 1), 0) + ti * tm + 1).astype(F32)
    ys = []
    for gi, w in enumerate(POOL_WINDOWS):
        cols = slice(gi * GROUP, (gi + 1) * GROUP)
        s = ext[:, cols]
        width = 1
        while width < w:
            s = s + pltpu.roll(s, width, 0)
            width *= 2
        inv = 1.0 / jnp.minimum(frames, float(w))
        d = s[POOL_HALO:, :] * inv - xp[:, cols]
        ys.append(jnp.dot(d.astype(BF16), pool_w_ref[gi], preferred_element_type=F32))
    y_a = jnp.concatenate(ys, axis=-1) * pool_scale_ref[...]

    lbr = lb_ref[...]
    mx = jnp.max(lbr, axis=0, keepdims=True)
    e = jnp.exp(lbr - mx)
    lb = e[0:1, :] / jnp.sum(e, axis=0, keepdims=True)
    fgt = lb + (1.0 - lb) * jax.nn.sigmoid(f_pre)
    lf = jnp.log(fgt)
    kk = 1.0 - fgt
    qf = q * jax.nn.sigmoid(q)

    r_i = lax.broadcasted_iota(jnp.int32, (tm, tm), 0)
    c_i = lax.broadcasted_iota(jnp.int32, (tm, tm), 1)
    tri = jnp.where((c_i <= r_i) & ((r_i // CHUNK) == (c_i // CHUNK)), 1.0, 0.0).astype(BF16)
    b = sum(jnp.dot(tri, p, preferred_element_type=F32) for p in _split3_bf16(lf))

    row = lax.broadcasted_iota(jnp.int32, (tm, 1), 0)
    qt, kt = [], []
    for half in LEVELS:
        sc = _level_scale(half, b, fgt, row)
        qt.append((qf * sc).astype(BF16))
        kt.append((kk * sc).astype(BF16))

    t_i = lax.broadcasted_iota(jnp.int32, (CHUNK, CHUNK), 0)
    s_i = lax.broadcasted_iota(jnp.int32, (CHUNK, CHUNK), 1)
    masks = [((t_i & half) != 0) & ((s_i & half) == 0)
             & ((t_i // (2 * half)) == (s_i // (2 * half))) for half in LEVELS]

    qk = qf * kk
    qd = (qf * jnp.exp(b)).astype(BF16)
    vb = v.astype(BF16)
    gn = gnorm_ref[...]

    o_heads = []
    for hd in range(heads):
        cols = slice(hd * GROUP, (hd + 1) * GROUP)
        st = st_ref[hd]
        diag = jnp.sum(qk[:, cols], axis=-1, keepdims=True)
        o_chunks = []
        for c in range(tm // CHUNK):
            rows = slice(c * CHUNK, (c + 1) * CHUNK)
            a = jnp.zeros((CHUNK, CHUNK), F32)
            for li in range(len(LEVELS)):
                p = lax.dot_general(qt[li][rows, cols], kt[li][rows, cols], NT_DIMS,
                                    preferred_element_type=F32)
                a = a + jnp.where(masks[li], p, 0.0)
            v_c = vb[rows, cols]
            o = jnp.dot(a.astype(BF16), v_c, preferred_element_type=F32)
            o = o + diag[rows, :] * v[rows, cols]
            o = o + lax.dot_general(qd[rows, cols], st.astype(BF16), NT_DIMS,
                                    preferred_element_type=F32)
            b_c = b[rows, cols]
            b_last = b_c[CHUNK - 1:CHUNK, :]
            kd = (kk[rows, cols] * jnp.exp(b_last - b_c)).astype(BF16)
            st = jnp.exp(b_last) * st + lax.dot_general(v_c, kd, TN_DIMS,
                                                        preferred_element_type=F32)
            o_chunks.append(o)
        st_ref[hd] = st
        o_h = jnp.concatenate(o_chunks, axis=0)
        ms = jnp.mean(o_h * o_h, axis=-1, keepdims=True)
        g_h = g[:, cols]
        o_heads.append(o_h * lax.rsqrt(ms + RMS_EPS) * gn * (g_h * jax.nn.sigmoid(g_h)))
    y_b = jnp.concatenate(o_heads, axis=-1)

    y = jnp.concatenate([y_a, y_b], axis=-1).astype(BF16)
    mix = jnp.dot(y, w_out_ref[...], preferred_element_type=F32)
    out_ref[0] = _layer_norm(alpha * x + mix, g1_ref[...], b1_ref[...])


def _gelu_tanh(a):
    c = 0.7978845608028654
    return 0.5 * a * (1.0 + jnp.tanh(c * (a + 0.044715 * (a * a * a))))


def _ffn_kernel(h_ref, w_up_ref, conv_w_ref, conv_b_ref, w_down_ref, g2_ref, b2_ref,
                out_ref, halo_ref, *, tm, alpha, d_ff):
    ti = pl.program_id(1)

    @pl.when(ti == 0)
    def _():
        halo_ref[...] = jnp.zeros_like(halo_ref)

    h = h_ref[0]
    hb = h.astype(BF16)
    acc = jnp.zeros(h.shape, F32)

    def conv(u, cols):
        ext = jnp.concatenate([halo_ref[:, cols], u], axis=0)
        halo_ref[:, cols] = u[tm - CONV_HALO:, :]
        u1 = pltpu.roll(ext, 1, 0)[CONV_HALO:, :]
        u2 = pltpu.roll(ext, 2, 0)[CONV_HALO:, :]
        cw = conv_w_ref[:, cols]
        return (conv_b_ref[:, cols] + cw[0:1, :] * u2 + cw[1:2, :] * u1 + cw[2:3, :] * u)

    for j in range(d_ff // FF_CHUNK):
        ca = slice(j * FF_CHUNK, (j + 1) * FF_CHUNK)
        cv = slice(d_ff + j * FF_CHUNK, d_ff + (j + 1) * FF_CHUNK)
        ua = conv(jnp.dot(hb, w_up_ref[:, ca], preferred_element_type=F32), ca)
        uv = conv(jnp.dot(hb, w_up_ref[:, cv], preferred_element_type=F32), cv)
        gated = (_gelu_tanh(ua) * uv).astype(BF16)
        acc = acc + jnp.dot(gated, w_down_ref[ca, :], preferred_element_type=F32)

    out_ref[0] = _layer_norm(alpha * h + acc, g2_ref[...], b2_ref[...])


def _const_spec(shape):
    nd = len(shape)
    return pl.BlockSpec(shape, lambda b, t: (0,) * nd, pipeline_mode=pl.Buffered(1))


def _mixer_ln1(x, w_in, pool_w, pool_scale, lb, gnorm, w_out, g1, b1, *, tm, alpha):
    bsz, seq, d = x.shape
    heads = lb.shape[1] // GROUP
    kernel = functools.partial(_mixer_kernel, tm=tm, alpha=alpha, heads=heads)
    consts = (w_in, pool_w, pool_scale, lb, gnorm, w_out, g1, b1)
    return pl.pallas_call(
        kernel,
        grid=(bsz, seq // tm),
        in_specs=[pl.BlockSpec((1, tm, d), lambda b, t: (b, t, 0))]
        + [_const_spec(c.shape) for c in consts],
        out_specs=pl.BlockSpec((1, tm, d), lambda b, t: (b, t, 0)),
        out_shape=jax.ShapeDtypeStruct(x.shape, F32),
        scratch_shapes=[
            pltpu.VMEM((heads, GROUP, GROUP), F32),
            pltpu.VMEM((POOL_HALO, GROUP * len(POOL_WINDOWS)), F32),
        ],
        compiler_params=pltpu.CompilerParams(
            dimension_semantics=("arbitrary", "arbitrary"),
            vmem_limit_bytes=48 * 1024 * 1024),
        name="mixer_ln1",
    )(x, *consts)


def _ffn_ln2(h, w_up, conv_w, conv_b, w_down, g2, b2, *, tm, alpha):
    bsz, seq, d = h.shape
    d_ff = w_down.shape[0]
    kernel = functools.partial(_ffn_kernel, tm=tm, alpha=alpha, d_ff=d_ff)
    consts = (w_up, conv_w, conv_b, w_down, g2, b2)
    return pl.pallas_call(
        kernel,
        grid=(bsz, seq // tm),
        in_specs=[pl.BlockSpec((1, tm, d), lambda b, t: (b, t, 0))]
        + [_const_spec(c.shape) for c in consts],
        out_specs=pl.BlockSpec((1, tm, d), lambda b, t: (b, t, 0)),
        out_shape=jax.ShapeDtypeStruct(h.shape, F32),
        scratch_shapes=[pltpu.VMEM((CONV_HALO, 2 * d_ff), F32)],
        compiler_params=pltpu.CompilerParams(
            dimension_semantics=("arbitrary", "arbitrary"),
            vmem_limit_bytes=56 * 1024 * 1024),
        name="ffn_ln2",
    )(h, *consts)


def kernel(x, w_in, pool_w, pool_scale, hgrn_lb, hgrn_gnorm, w_out, ln1_g, ln1_b,
           w_up, conv_w, conv_b, w_down, ln2_g, ln2_b):
    depth = w_in.shape[0]
    assert depth == 1 and hgrn_lb.shape[0] == 2, "single-layer block expected"
    assert w_down.shape[1] % FF_CHUNK == 0
    alpha = (2.0 * depth) ** 0.25
    h = _mixer_ln1(
        x, w_in[0].astype(BF16), pool_w[0].astype(BF16), pool_scale, hgrn_lb,
        hgrn_gnorm, w_out[0].astype(BF16), ln1_g, ln1_b, tm=256, alpha=alpha)
    return _ffn_ln2(
        h, w_up[0].astype(BF16), conv_w[0], conv_b, w_down[0].astype(BF16),
        ln2_g, ln2_b, tm=256, alpha=alpha)
```

```python
import functools

import jax
import jax.numpy as jnp
from jax import lax
from jax.experimental import pallas as pl
from jax.experimental.pallas import tpu as pltpu

F32 = jnp.float32
BF16 = jnp.bfloat16

CHUNK = 64
POOL_WINDOWS = (2, 4, 8, 16)
GROUP = 128
POOL_HALO = 16
CONV_WIDTH = 3
CONV_HALO = 8
FF_CHUNK = 256
LN_EPS = 1e-5
RMS_EPS = 1e-6
LEVELS = (32, 16, 8, 4, 2, 1)

NT_DIMS = (((1,), (1,)), ((), ()))
TN_DIMS = (((0,), (0,)), ((), ()))


def _layer_norm(z, g, b):
    mu = jnp.mean(z, axis=-1, keepdims=True)
    zc = z - mu
    var = jnp.mean(zc * zc, axis=-1, keepdims=True)
    return zc * lax.rsqrt(var + LN_EPS) * g + b


def _split3_bf16(a):
    hi = a.astype(BF16)
    r1 = a - hi.astype(F32)
    mid = r1.astype(BF16)
    lo = (r1 - mid.astype(F32)).astype(BF16)
    return hi, mid, lo


def _group_ref_rows(b, half):
    n, c = b.shape
    g = 2 * half
    b3 = b.reshape(n // g, g, c)
    ref = jnp.broadcast_to(b3[:, half - 1:half, :], b3.shape)
    return ref.reshape(n, c)


def _level_scale(half, b, fgt, row):
    n = b.shape[0]
    if half == 1:
        return jnp.where((row & 1) == 1, fgt, 1.0)
    if half == 2:
        prev = pltpu.roll(fgt, 1, 0)
        nxt = pltpu.roll(fgt, n - 1, 0)
        j = row & 3
        return jnp.where(j == 0, nxt,
                         jnp.where(j == 1, 1.0,
                                   jnp.where(j == 2, fgt, fgt * prev)))
    ref = _group_ref_rows(b, half)
    later = (row & half) != 0
    return jnp.exp(jnp.where(later, b - ref, ref - b))


def _mixer_kernel(x_ref, w_in_ref, pool_w_ref, pool_scale_ref, lb_ref, gnorm_ref,
                  w_out_ref, g1_ref, b1_ref, out_ref, st_ref, halo_ref,
                  *, tm, alpha, heads):
    ti = pl.program_id(1)
    pw = GROUP * len(POOL_WINDOWS)
    hw = GROUP * heads

    @pl.when(ti == 0)
    def _():
        st_ref[...] = jnp.zeros_like(st_ref)
        halo_ref[...] = jnp.zeros_like(halo_ref)

    x = x_ref[0]
    proj = jnp.dot(x.astype(BF16), w_in_ref[...], preferred_element_type=F32)
    xp = proj[:, :pw]
    q = proj[:, pw:pw + hw]
    f_pre = proj[:, pw + hw:pw + 2 * hw]
    v = proj[:, pw + 2 * hw:pw + 3 * hw]
    g = proj[:, pw + 3 * hw:pw + 4 * hw]

    ext = jnp.concatenate([halo_ref[...], xp], axis=0)
    halo_ref[...] = xp[tm - POOL_HALO:, :]
    frames = (lax.broadcasted_iota(jnp.int32, (tm, 1), 0) + ti * tm + 1).astype(F32)
    ys = []
    for gi, w in enumerate(POOL_WINDOWS):
        cols = slice(gi * GROUP, (gi + 1) * GROUP)
        s = ext[:, cols]
        width = 1
        while width < w:
            s = s + pltpu.roll(s, width, 0)
            width *= 2
        inv = 1.0 / jnp.minimum(frames, float(w))
        d = s[POOL_HALO:, :] * inv - xp[:, cols]
        ys.append(jnp.dot(d.astype(BF16), pool_w_ref[gi], preferred_element_type=F32))
    y_a = jnp.concatenate(ys, axis=-1) * pool_scale_ref[...]

    lbr = lb_ref[...]
    mx = jnp.max(lbr, axis=0, keepdims=True)
    e = jnp.exp(lbr - mx)
    lb = e[0:1, :] / jnp.sum(e, axis=0, keepdims=True)
    fgt = lb + (1.0 - lb) * jax.nn.sigmoid(f_pre)
    lf = jnp.log(fgt)
    kk = 1.0 - fgt
    qf = q * jax.nn.sigmoid(q)

    r_i = lax.broadcasted_iota(jnp.int32, (tm, tm), 0)
    c_i = lax.broadcasted_iota(jnp.int32, (tm, tm), 1)
    tri = jnp.where((c_i <= r_i) & ((r_i // CHUNK) == (c_i // CHUNK)), 1.0, 0.0).astype(BF16)
    b = sum(jnp.dot(tri, p, preferred_element_type=F32) for p in _split3_bf16(lf))

    row = lax.broadcasted_iota(jnp.int32, (tm, 1), 0)
    qt, kt = [], []
    for half in LEVELS:
        sc = _level_scale(half, b, fgt, row)
        qt.append((qf * sc).astype(BF16))
        kt.append((kk * sc).astype(BF16))

    t_i = lax.broadcasted_iota(jnp.int32, (CHUNK, CHUNK), 0)
    s_i = lax.broadcasted_iota(jnp.int32, (CHUNK, CHUNK), 1)
    masks = [((t_i & half) != 0) & ((s_i & half) == 0)
             & ((t_i // (2 * half)) == (s_i // (2 * half))) for half in LEVELS]

    qk = qf * kk
    qd = (qf * jnp.exp(b)).astype(BF16)
    vb = v.astype(BF16)
    gn = gnorm_ref[...]

    o_heads = []
    for hd in range(heads):
        cols = slice(hd * GROUP, (hd + 1) * GROUP)
        st = st_ref[hd]
        diag = jnp.sum(qk[:, cols], axis=-1, keepdims=True)
        o_chunks = []
        for c in range(tm // CHUNK):
            rows = slice(c * CHUNK, (c + 1) * CHUNK)
            a = jnp.zeros((CHUNK, CHUNK), F32)
            for li in range(len(LEVELS)):
                p = lax.dot_general(qt[li][rows, cols], kt[li][rows, cols], NT_DIMS,
                                    preferred_element_type=F32)
                a = a + jnp.where(masks[li], p, 0.0)
            v_c = vb[rows, cols]
            o = jnp.dot(a.astype(BF16), v_c, preferred_element_type=F32)
            o = o + diag[rows, :] * v[rows, cols]
            o = o + lax.dot_general(qd[rows, cols], st.astype(BF16), NT_DIMS,
                                    preferred_element_type=F32)
            b_c = b[rows, cols]
            b_last = b_c[CHUNK - 1:CHUNK, :]
            kd = (kk[rows, cols] * jnp.exp(b_last - b_c)).astype(BF16)
            st = jnp.exp(b_last) * st + lax.dot_general(v_c, kd, TN_DIMS,
                                                        preferred_element_type=F32)
            o_chunks.append(o)
        st_ref[hd] = st
        o_h = jnp.concatenate(o_chunks, axis=0)
        ms = jnp.mean(o_h * o_h, axis=-1, keepdims=True)
        g_h = g[:, cols]
        o_heads.append(o_h * lax.rsqrt(ms + RMS_EPS) * gn * (g_h * jax.nn.sigmoid(g_h)))
    y_b = jnp.concatenate(o_heads, axis=-1)

    y = jnp.concatenate([y_a, y_b], axis=-1).astype(BF16)
    mix = jnp.dot(y, w_out_ref[...], preferred_element_type=F32)
    out_ref[0] = _layer_norm(alpha * x + mix, g1_ref[...], b1_ref[...])


def _gelu_tanh(a):
    c = 0.7978845608028654
    return 0.5 * a * (1.0 + jnp.tanh(c * (a + 0.044715 * (a * a * a))))


def _ffn_kernel(h_ref, w_up_ref, conv_w_ref, conv_b_ref, w_down_ref, g2_ref, b2_ref,
                out_ref, halo_ref, pin_ref, pout_ref, *, tm, alpha, d_ff):
    ti = pl.program_id(1)
    r = tm // 8

    @pl.when(ti == 0)
    def _():
        halo_ref[...] = jnp.zeros_like(halo_ref)

    n_cb = h_ref.shape[2] // GROUP
    for c in range(n_cb):
        pin_ref[c] = h_ref[0, :, c * GROUP:(c + 1) * GROUP]
    h = jnp.concatenate(
        [jnp.concatenate([pin_ref[c, pl.ds(s, r, stride=8), :] for c in range(n_cb)], axis=1)
         for s in range(8)], axis=0)
    hb = h.astype(BF16)
    acc = jnp.zeros(h.shape, F32)

    def up(cols):
        return jnp.dot(hb, w_up_ref[:, cols], preferred_element_type=F32)

    def conv(u, cols):
        shifted = []
        for k in range(CONV_WIDTH - 1):
            blk = u[(6 + k) * r:(7 + k) * r, :]
            ext = jnp.concatenate([halo_ref[k, :, cols], blk], axis=0)
            halo_ref[k, :, cols] = blk[r - 8:, :]
            shifted.append(pltpu.roll(ext, 1, 0)[8:, :])
        u1 = jnp.concatenate([shifted[1], u[:7 * r, :]], axis=0)
        u2 = jnp.concatenate(shifted + [u[:6 * r, :]], axis=0)
        cw = conv_w_ref[:, cols]
        return conv_b_ref[:, cols] + cw[0:1, :] * u2 + cw[1:2, :] * u1 + cw[2:3, :] * u

    n_chunks = d_ff // FF_CHUNK
    chunks = [(slice(j * FF_CHUNK, (j + 1) * FF_CHUNK),
               slice(d_ff + j * FF_CHUNK, d_ff + (j + 1) * FF_CHUNK)) for j in range(n_chunks)]
    nxt = (up(chunks[0][0]), up(chunks[0][1]))
    for j, (ca, cv) in enumerate(chunks):
        ua, uv = nxt
        if j + 1 < n_chunks:
            nxt = (up(chunks[j + 1][0]), up(chunks[j + 1][1]))
        gated = (_gelu_tanh(conv(ua, ca)) * conv(uv, cv)).astype(BF16)
        acc = acc + jnp.dot(gated, w_down_ref[ca, :], preferred_element_type=F32)

    res = _layer_norm(alpha * h + acc, g2_ref[...], b2_ref[...])
    for c in range(n_cb):
        for s in range(8):
            pout_ref[c, pl.ds(s, r, stride=8), :] = res[s * r:(s + 1) * r, c * GROUP:(c + 1) * GROUP]
    for c in range(n_cb):
        out_ref[0, :, c * GROUP:(c + 1) * GROUP] = pout_ref[c]


def _const_spec(shape):
    nd = len(shape)
    return pl.BlockSpec(shape, lambda b, t: (0,) * nd, pipeline_mode=pl.Buffered(1))


def _mixer_ln1(x, w_in, pool_w, pool_scale, lb, gnorm, w_out, g1, b1, *, tm, alpha):
    bsz, seq, d = x.shape
    heads = lb.shape[1] // GROUP
    kernel = functools.partial(_mixer_kernel, tm=tm, alpha=alpha, heads=heads)
    consts = (w_in, pool_w, pool_scale, lb, gnorm, w_out, g1, b1)
    return pl.pallas_call(
        kernel,
        grid=(bsz, seq // tm),
        in_specs=[pl.BlockSpec((1, tm, d), lambda b, t: (b, t, 0))]
        + [_const_spec(c.shape) for c in consts],
        out_specs=pl.BlockSpec((1, tm, d), lambda b, t: (b, t, 0)),
        out_shape=jax.ShapeDtypeStruct(x.shape, F32),
        scratch_shapes=[
            pltpu.VMEM((heads, GROUP, GROUP), F32),
            pltpu.VMEM((POOL_HALO, GROUP * len(POOL_WINDOWS)), F32),
        ],
        compiler_params=pltpu.CompilerParams(
            dimension_semantics=("arbitrary", "arbitrary"),
            vmem_limit_bytes=48 * 1024 * 1024),
        name="mixer_ln1",
    )(x, *consts)


def _ffn_ln2(h, w_up, conv_w, conv_b, w_down, g2, b2, *, tm, alpha):
    bsz, seq, d = h.shape
    d_ff = w_down.shape[0]
    kernel = functools.partial(_ffn_kernel, tm=tm, alpha=alpha, d_ff=d_ff)
    consts = (w_up, conv_w, conv_b, w_down, g2, b2)
    return pl.pallas_call(
        kernel,
        grid=(bsz, seq // tm),
        in_specs=[pl.BlockSpec((1, tm, d), lambda b, t: (b, t, 0))]
        + [_const_spec(c.shape) for c in consts],
        out_specs=pl.BlockSpec((1, tm, d), lambda b, t: (b, t, 0)),
        out_shape=jax.ShapeDtypeStruct(h.shape, F32),
        scratch_shapes=[
            pltpu.VMEM((CONV_WIDTH - 1, CONV_HALO, 2 * d_ff), F32),
            pltpu.VMEM((d // GROUP, tm, GROUP), F32),
            pltpu.VMEM((d // GROUP, tm, GROUP), F32),
        ],
        compiler_params=pltpu.CompilerParams(
            dimension_semantics=("arbitrary", "arbitrary"),
            vmem_limit_bytes=56 * 1024 * 1024),
        name="ffn_ln2",
    )(h, *consts)


def kernel(x, w_in, pool_w, pool_scale, hgrn_lb, hgrn_gnorm, w_out, ln1_g, ln1_b,
           w_up, conv_w, conv_b, w_down, ln2_g, ln2_b):
    depth = w_in.shape[0]
    assert depth == 1 and hgrn_lb.shape[0] == 2, "single-layer block expected"
    assert w_down.shape[1] % FF_CHUNK == 0
    alpha = (2.0 * depth) ** 0.25
    h = _mixer_ln1(
        x, w_in[0].astype(BF16), pool_w[0].astype(BF16), pool_scale, hgrn_lb,
        hgrn_gnorm, w_out[0].astype(BF16), ln1_g, ln1_b, tm=256, alpha=alpha)
    return _ffn_ln2(
        h, w_up[0].astype(BF16), conv_w[0], conv_b, w_down[0].astype(BF16),
        ln2_g, ln2_b, tm=256, alpha=alpha)
```

```python
import functools

import jax
import jax.numpy as jnp
from jax import lax
from jax.experimental import pallas as pl
from jax.experimental.pallas import tpu as pltpu

F32 = jnp.float32
BF16 = jnp.bfloat16

STREAMS = 2
CHUNK = 64
POOL_WINDOWS = (2, 4, 8, 16)
GROUP = 128
POOL_HALO = 16
CONV_WIDTH = 3
CONV_HALO = 8
FF_CHUNK = 256
LN_EPS = 1e-5
RMS_EPS = 1e-6
LEVELS = (32, 16, 8, 4, 2, 1)

NT_DIMS = (((1,), (1,)), ((), ()))
TN_DIMS = (((0,), (0,)), ((), ()))


def _interleave(streams):
    live = list(streams)
    while live:
        still = []
        for s in live:
            try:
                next(s)
                still.append(s)
            except StopIteration:
                pass
        live = still


def _layer_norm(z, g, b):
    mu = jnp.mean(z, axis=-1, keepdims=True)
    zc = z - mu
    var = jnp.mean(zc * zc, axis=-1, keepdims=True)
    return zc * lax.rsqrt(var + LN_EPS) * g + b


def _split3_bf16(a):
    hi = a.astype(BF16)
    r1 = a - hi.astype(F32)
    mid = r1.astype(BF16)
    lo = (r1 - mid.astype(F32)).astype(BF16)
    return hi, mid, lo


def _group_ref_rows(b, half):
    n, c = b.shape
    g = 2 * half
    b3 = b.reshape(n // g, g, c)
    ref = jnp.broadcast_to(b3[:, half - 1:half, :], b3.shape)
    return ref.reshape(n, c)


def _level_scale(half, b, fgt, row):
    n = b.shape[0]
    if half == 1:
        return jnp.where((row & 1) == 1, fgt, 1.0)
    if half == 2:
        prev = pltpu.roll(fgt, 1, 0)
        nxt = pltpu.roll(fgt, n - 1, 0)
        j = row & 3
        return jnp.where(j == 0, nxt,
                         jnp.where(j == 1, 1.0,
                                   jnp.where(j == 2, fgt, fgt * prev)))
    ref = _group_ref_rows(b, half)
    later = (row & half) != 0
    return jnp.exp(jnp.where(later, b - ref, ref - b))


def _mixer_kernel(x_ref, w_in_ref, pool_w_ref, pool_scale_ref, lb_ref, gnorm_ref,
                  w_out_ref, g1_ref, b1_ref, out_ref, st_ref, halo_ref,
                  *, tm, alpha, heads):
    ti = pl.program_id(1)
    pw = GROUP * len(POOL_WINDOWS)
    hw = GROUP * heads
    n_chunks = tm // CHUNK

    @pl.when(ti == 0)
    def _():
        st_ref[...] = jnp.zeros_like(st_ref)
        halo_ref[...] = jnp.zeros_like(halo_ref)

    lbr = lb_ref[...]
    mx = jnp.max(lbr, axis=0, keepdims=True)
    e = jnp.exp(lbr - mx)
    lb = e[0:1, :] / jnp.sum(e, axis=0, keepdims=True)
    frames = (lax.broadcasted_iota(jnp.int32, (tm, 1), 0) + ti * tm + 1).astype(F32)
    inv_cnt = [1.0 / jnp.minimum(frames, float(w)) for w in POOL_WINDOWS]
    r_i = lax.broadcasted_iota(jnp.int32, (tm, tm), 0)
    c_i = lax.broadcasted_iota(jnp.int32, (tm, tm), 1)
    tri = jnp.where((c_i <= r_i) & ((r_i // CHUNK) == (c_i // CHUNK)), 1.0, 0.0).astype(BF16)
    row = lax.broadcasted_iota(jnp.int32, (tm, 1), 0)
    t_i = lax.broadcasted_iota(jnp.int32, (CHUNK, CHUNK), 0)
    s_i = lax.broadcasted_iota(jnp.int32, (CHUNK, CHUNK), 1)
    masks = [((t_i & half) != 0) & ((s_i & half) == 0)
             & ((t_i // (2 * half)) == (s_i // (2 * half))) for half in LEVELS]
    gn = gnorm_ref[...]

    def stream(si):
        x = x_ref[si]
        proj = jnp.dot(x.astype(BF16), w_in_ref[...], preferred_element_type=F32)
        xp = proj[:, :pw]
        q = proj[:, pw:pw + hw]
        f_pre = proj[:, pw + hw:pw + 2 * hw]
        v = proj[:, pw + 2 * hw:pw + 3 * hw]
        g = proj[:, pw + 3 * hw:pw + 4 * hw]
        yield

        ext = jnp.concatenate([halo_ref[si], xp], axis=0)
        halo_ref[si] = xp[tm - POOL_HALO:, :]
        ds = []
        for gi, w in enumerate(POOL_WINDOWS):
            cols = slice(gi * GROUP, (gi + 1) * GROUP)
            s = ext[:, cols]
            width = 1
            while width < w:
                s = s + pltpu.roll(s, width, 0)
                width *= 2
            ds.append((s[POOL_HALO:, :] * inv_cnt[gi] - xp[:, cols]).astype(BF16))
        fgt = lb + (1.0 - lb) * jax.nn.sigmoid(f_pre)
        lf = jnp.log(fgt)
        kk = 1.0 - fgt
        qf = q * jax.nn.sigmoid(q)
        pieces = _split3_bf16(lf)
        yield

        b = sum(jnp.dot(tri, p, preferred_element_type=F32) for p in pieces)
        y_a = jnp.concatenate(
            [jnp.dot(ds[gi], pool_w_ref[gi], preferred_element_type=F32)
             for gi in range(len(POOL_WINDOWS))], axis=-1) * pool_scale_ref[...]
        yield

        qt, kt = [], []
        for half in LEVELS:
            sc = _level_scale(half, b, fgt, row)
            qt.append((qf * sc).astype(BF16))
            kt.append((kk * sc).astype(BF16))
        qk = qf * kk
        qd = (qf * jnp.exp(b)).astype(BF16)
        vb = v.astype(BF16)
        kd, dec = [], []
        for c in range(n_chunks):
            b_c = b[c * CHUNK:(c + 1) * CHUNK, :]
            b_last = b_c[CHUNK - 1:CHUNK, :]
            kd.append((kk[c * CHUNK:(c + 1) * CHUNK, :] * jnp.exp(b_last - b_c)).astype(BF16))
            dec.append(jnp.exp(b_last))
        yield

        p_lv = {}
        upd = {}
        for c in range(n_chunks):
            rows = slice(c * CHUNK, (c + 1) * CHUNK)
            for hd in range(heads):
                cols = slice(hd * GROUP, (hd + 1) * GROUP)
                p_lv[c, hd] = [lax.dot_general(qt[li][rows, cols], kt[li][rows, cols], NT_DIMS,
                                               preferred_element_type=F32)
                               for li in range(len(LEVELS))]
                upd[c, hd] = lax.dot_general(vb[rows, cols], kd[c][:, cols], TN_DIMS,
                                             preferred_element_type=F32)
            yield

        o_intra = {}
        for c in range(n_chunks):
            rows = slice(c * CHUNK, (c + 1) * CHUNK)
            for hd in range(heads):
                cols = slice(hd * GROUP, (hd + 1) * GROUP)
                a = jnp.zeros((CHUNK, CHUNK), F32)
                for li in range(len(LEVELS)):
                    a = a + jnp.where(masks[li], p_lv[c, hd][li], 0.0)
                o_intra[c, hd] = jnp.dot(a.astype(BF16), vb[rows, cols],
                                         preferred_element_type=F32)
            yield

        o_inter = {}
        for hd in range(heads):
            cols = slice(hd * GROUP, (hd + 1) * GROUP)
            st = st_ref[si, hd]
            for c in range(n_chunks):
                rows = slice(c * CHUNK, (c + 1) * CHUNK)
                o_inter[c, hd] = lax.dot_general(qd[rows, cols], st.astype(BF16), NT_DIMS,
                                                 preferred_element_type=F32)
                st = dec[c][:, cols] * st + upd[c, hd]
            st_ref[si, hd] = st
        yield

        o_heads = []
        for hd in range(heads):
            cols = slice(hd * GROUP, (hd + 1) * GROUP)
            diag = jnp.sum(qk[:, cols], axis=-1, keepdims=True)
            o_h = jnp.concatenate([o_intra[c, hd] + o_inter[c, hd] for c in range(n_chunks)],
                                  axis=0) + diag * v[:, cols]
            ms = jnp.mean(o_h * o_h, axis=-1, keepdims=True)
            g_h = g[:, cols]
            o_heads.append(o_h * lax.rsqrt(ms + RMS_EPS) * gn * (g_h * jax.nn.sigmoid(g_h)))
        y = jnp.concatenate([y_a] + o_heads, axis=-1).astype(BF16)
        yield

        mix = jnp.dot(y, w_out_ref[...], preferred_element_type=F32)
        yield
        out_ref[si] = _layer_norm(alpha * x + mix, g1_ref[...], b1_ref[...])

    _interleave([stream(si) for si in range(x_ref.shape[0])])


def _gelu_tanh(a):
    c = 0.7978845608028654
    return 0.5 * a * (1.0 + jnp.tanh(c * (a + 0.044715 * (a * a * a))))


def _ffn_kernel(h_ref, w_up_ref, conv_w_ref, conv_b_ref, w_down_ref, g2_ref, b2_ref,
                out_ref, halo_ref, pin_ref, pout_ref, *, tm, alpha, d_ff):
    ti = pl.program_id(1)
    r = tm // 8
    n_cb = h_ref.shape[2] // GROUP
    n_chunks = d_ff // FF_CHUNK
    chunks = [(slice(j * FF_CHUNK, (j + 1) * FF_CHUNK),
               slice(d_ff + j * FF_CHUNK, d_ff + (j + 1) * FF_CHUNK)) for j in range(n_chunks)]

    @pl.when(ti == 0)
    def _():
        halo_ref[...] = jnp.zeros_like(halo_ref)

    def stream(si):
        for c in range(n_cb):
            pin_ref[si, c] = h_ref[si, :, c * GROUP:(c + 1) * GROUP]
        h = jnp.concatenate(
            [jnp.concatenate([pin_ref[si, c, pl.ds(s, r, stride=8), :] for c in range(n_cb)],
                             axis=1) for s in range(8)], axis=0)
        hb = h.astype(BF16)
        acc = jnp.zeros(h.shape, F32)

        def up(cols):
            return jnp.dot(hb, w_up_ref[:, cols], preferred_element_type=F32)

        def conv(u, cols):
            shifted = []
            for k in range(CONV_WIDTH - 1):
                blk = u[(6 + k) * r:(7 + k) * r, :]
                ext = jnp.concatenate([halo_ref[si, k, :, cols], blk], axis=0)
                halo_ref[si, k, :, cols] = blk[r - 8:, :]
                shifted.append(pltpu.roll(ext, 1, 0)[8:, :])
            u1 = jnp.concatenate([shifted[1], u[:7 * r, :]], axis=0)
            u2 = jnp.concatenate(shifted + [u[:6 * r, :]], axis=0)
            cw = conv_w_ref[:, cols]
            return conv_b_ref[:, cols] + cw[0:1, :] * u2 + cw[1:2, :] * u1 + cw[2:3, :] * u

        nxt = (up(chunks[0][0]), up(chunks[0][1]))
        yield
        for j, (ca, cv) in enumerate(chunks):
            ua, uv = nxt
            if j + 1 < n_chunks:
                nxt = (up(chunks[j + 1][0]), up(chunks[j + 1][1]))
                yield
            gated = (_gelu_tanh(conv(ua, ca)) * conv(uv, cv)).astype(BF16)
            acc = acc + jnp.dot(gated, w_down_ref[ca, :], preferred_element_type=F32)
            yield

        res = _layer_norm(alpha * h + acc, g2_ref[...], b2_ref[...])
        for c in range(n_cb):
            for s in range(8):
                pout_ref[si, c, pl.ds(s, r, stride=8), :] = (
                    res[s * r:(s + 1) * r, c * GROUP:(c + 1) * GROUP])
        for c in range(n_cb):
            out_ref[si, :, c * GROUP:(c + 1) * GROUP] = pout_ref[si, c]

    _interleave([stream(si) for si in range(h_ref.shape[0])])


def _const_spec(shape):
    nd = len(shape)
    return pl.BlockSpec(shape, lambda b, t: (0,) * nd, pipeline_mode=pl.Buffered(1))


def _mixer_ln1(x, w_in, pool_w, pool_scale, lb, gnorm, w_out, g1, b1, *, tm, alpha):
    bsz, seq, d = x.shape
    heads = lb.shape[1] // GROUP
    kernel = functools.partial(_mixer_kernel, tm=tm, alpha=alpha, heads=heads)
    consts = (w_in, pool_w, pool_scale, lb, gnorm, w_out, g1, b1)
    return pl.pallas_call(
        kernel,
        grid=(bsz // STREAMS, seq // tm),
        in_specs=[pl.BlockSpec((STREAMS, tm, d), lambda b, t: (b, t, 0))]
        + [_const_spec(c.shape) for c in consts],
        out_specs=pl.BlockSpec((STREAMS, tm, d), lambda b, t: (b, t, 0)),
        out_shape=jax.ShapeDtypeStruct(x.shape, F32),
        scratch_shapes=[
            pltpu.VMEM((STREAMS, heads, GROUP, GROUP), F32),
            pltpu.VMEM((STREAMS, POOL_HALO, GROUP * len(POOL_WINDOWS)), F32),
        ],
        compiler_params=pltpu.CompilerParams(
            dimension_semantics=("arbitrary", "arbitrary"),
            vmem_limit_bytes=48 * 1024 * 1024),
        name="mixer_ln1",
    )(x, *consts)


def _ffn_ln2(h, w_up, conv_w, conv_b, w_down, g2, b2, *, tm, alpha):
    bsz, seq, d = h.shape
    d_ff = w_down.shape[0]
    kernel = functools.partial(_ffn_kernel, tm=tm, alpha=alpha, d_ff=d_ff)
    consts = (w_up, conv_w, conv_b, w_down, g2, b2)
    return pl.pallas_call(
        kernel,
        grid=(bsz // STREAMS, seq // tm),
        in_specs=[pl.BlockSpec((STREAMS, tm, d), lambda b, t: (b, t, 0))]
        + [_const_spec(c.shape) for c in consts],
        out_specs=pl.BlockSpec((STREAMS, tm, d), lambda b, t: (b, t, 0)),
        out_shape=jax.ShapeDtypeStruct(h.shape, F32),
        scratch_shapes=[
            pltpu.VMEM((STREAMS, CONV_WIDTH - 1, CONV_HALO, 2 * d_ff), F32),
            pltpu.VMEM((STREAMS, d // GROUP, tm, GROUP), F32),
            pltpu.VMEM((STREAMS, d // GROUP, tm, GROUP), F32),
        ],
        compiler_params=pltpu.CompilerParams(
            dimension_semantics=("arbitrary", "arbitrary"),
            vmem_limit_bytes=56 * 1024 * 1024),
        name="ffn_ln2",
    )(h, *consts)


def kernel(x, w_in, pool_w, pool_scale, hgrn_lb, hgrn_gnorm, w_out, ln1_g, ln1_b,
           w_up, conv_w, conv_b, w_down, ln2_g, ln2_b):
    depth = w_in.shape[0]
    assert depth == 1 and hgrn_lb.shape[0] == 2, "single-layer block expected"
    assert w_down.shape[1] % FF_CHUNK == 0 and x.shape[0] % STREAMS == 0
    alpha = (2.0 * depth) ** 0.25
    h = _mixer_ln1(
        x, w_in[0].astype(BF16), pool_w[0].astype(BF16), pool_scale, hgrn_lb,
        hgrn_gnorm, w_out[0].astype(BF16), ln1_g, ln1_b, tm=256, alpha=alpha)
    return _ffn_ln2(
        h, w_up[0].astype(BF16), conv_w[0], conv_b, w_down[0].astype(BF16),
        ln2_g, ln2_b, tm=256, alpha=alpha)
```

```python
import functools

import jax
import jax.numpy as jnp
from jax import lax
from jax.experimental import pallas as pl
from jax.experimental.pallas import tpu as pltpu

F32 = jnp.float32
BF16 = jnp.bfloat16

STREAMS = 2
CHUNK = 64
POOL_WINDOWS = (2, 4, 8, 16)
GROUP = 128
POOL_HALO = 16
CONV_WIDTH = 3
CONV_HALO = 8
FF_CHUNK = 256
LN_EPS = 1e-5
RMS_EPS = 1e-6
LEVELS = (32, 16, 8, 4, 2, 1)

NT_DIMS = (((1,), (1,)), ((), ()))
TN_DIMS = (((0,), (0,)), ((), ()))


def _interleave(streams):
    live = list(streams)
    while live:
        still = []
        for s in live:
            try:
                next(s)
                still.append(s)
            except StopIteration:
                pass
        live = still


def _layer_norm(z, g, b):
    mu = jnp.mean(z, axis=-1, keepdims=True)
    zc = z - mu
    var = jnp.mean(zc * zc, axis=-1, keepdims=True)
    return zc * lax.rsqrt(var + LN_EPS) * g + b


def _split3_bf16(a):
    hi = a.astype(BF16)
    r1 = a - hi.astype(F32)
    mid = r1.astype(BF16)
    lo = (r1 - mid.astype(F32)).astype(BF16)
    return hi, mid, lo


def _group_ref_rows(b, half):
    n, c = b.shape
    g = 2 * half
    b3 = b.reshape(n // g, g, c)
    ref = jnp.broadcast_to(b3[:, half - 1:half, :], b3.shape)
    return ref.reshape(n, c)


def _level_scale(half, b, fgt, row):
    n = b.shape[0]
    if half == 1:
        return jnp.where((row & 1) == 1, fgt, 1.0)
    if half == 2:
        prev = pltpu.roll(fgt, 1, 0)
        nxt = pltpu.roll(fgt, n - 1, 0)
        j = row & 3
        return jnp.where(j == 0, nxt,
                         jnp.where(j == 1, 1.0,
                                   jnp.where(j == 2, fgt, fgt * prev)))
    ref = _group_ref_rows(b, half)
    later = (row & half) != 0
    return jnp.exp(jnp.where(later, b - ref, ref - b))


def _mixer_kernel(x_ref, w_in_ref, pool_w_ref, pool_scale_ref, lb_ref, gnorm_ref,
                  w_out_ref, g1_ref, b1_ref, out_ref, st_ref, halo_ref,
                  *, tm, alpha, heads):
    ti = pl.program_id(1)
    pw = GROUP * len(POOL_WINDOWS)
    hw = GROUP * heads
    n_chunks = tm // CHUNK

    @pl.when(ti == 0)
    def _():
        st_ref[...] = jnp.zeros_like(st_ref)
        halo_ref[...] = jnp.zeros_like(halo_ref)

    lbr = lb_ref[...]
    mx = jnp.max(lbr, axis=0, keepdims=True)
    e = jnp.exp(lbr - mx)
    lb = e[0:1, :] / jnp.sum(e, axis=0, keepdims=True)
    frames = (lax.broadcasted_iota(jnp.int32, (tm, 1), 0) + ti * tm + 1).astype(F32)
    inv_cnt = [1.0 / jnp.minimum(frames, float(w)) for w in POOL_WINDOWS]
    r_i = lax.broadcasted_iota(jnp.int32, (tm, tm), 0)
    c_i = lax.broadcasted_iota(jnp.int32, (tm, tm), 1)
    tri = jnp.where((c_i <= r_i) & ((r_i // CHUNK) == (c_i // CHUNK)), 1.0, 0.0).astype(BF16)
    row = lax.broadcasted_iota(jnp.int32, (tm, 1), 0)
    t_i = lax.broadcasted_iota(jnp.int32, (CHUNK, CHUNK), 0)
    s_i = lax.broadcasted_iota(jnp.int32, (CHUNK, CHUNK), 1)
    masks = [((t_i & half) != 0) & ((s_i & half) == 0)
             & ((t_i // (2 * half)) == (s_i // (2 * half))) for half in LEVELS]
    gn = gnorm_ref[...]

    def stream(si):
        x = x_ref[si]
        proj = jnp.dot(x.astype(BF16), w_in_ref[...], preferred_element_type=F32)
        xp = proj[:, :pw]
        q = proj[:, pw:pw + hw]
        f_pre = proj[:, pw + hw:pw + 2 * hw]
        v = proj[:, pw + 2 * hw:pw + 3 * hw]
        g = proj[:, pw + 3 * hw:pw + 4 * hw]
        yield

        ext = jnp.concatenate([halo_ref[si], xp], axis=0)
        halo_ref[si] = xp[tm - POOL_HALO:, :]
        ds = []
        for gi, w in enumerate(POOL_WINDOWS):
            cols = slice(gi * GROUP, (gi + 1) * GROUP)
            s = ext[:, cols]
            width = 1
            while width < w:
                s = s + pltpu.roll(s, width, 0)
                width *= 2
            ds.append((s[POOL_HALO:, :] * inv_cnt[gi] - xp[:, cols]).astype(BF16))
        fgt = lb + (1.0 - lb) * jax.nn.sigmoid(f_pre)
        lf = jnp.log(fgt)
        kk = 1.0 - fgt
        qf = q * jax.nn.sigmoid(q)
        pieces = _split3_bf16(lf)
        yield

        b = sum(jnp.dot(tri, p, preferred_element_type=F32) for p in pieces)
        y_a = jnp.concatenate(
            [jnp.dot(ds[gi], pool_w_ref[gi], preferred_element_type=F32)
             for gi in range(len(POOL_WINDOWS))], axis=-1) * pool_scale_ref[...]
        yield

        qt, kt = [], []
        for half in LEVELS:
            sc = _level_scale(half, b, fgt, row)
            qt.append((qf * sc).astype(BF16))
            kt.append((kk * sc).astype(BF16))
        qk = qf * kk
        qd = (qf * jnp.exp(b)).astype(BF16)
        vb = v.astype(BF16)
        kd, dec = [], []
        for c in range(n_chunks):
            b_c = b[c * CHUNK:(c + 1) * CHUNK, :]
            b_last = b_c[CHUNK - 1:CHUNK, :]
            kd.append((kk[c * CHUNK:(c + 1) * CHUNK, :] * jnp.exp(b_last - b_c)).astype(BF16))
            dec.append(jnp.exp(b_last))
        yield

        p_lv = {}
        upd = {}
        for c in range(n_chunks):
            rows = slice(c * CHUNK, (c + 1) * CHUNK)
            for hd in range(heads):
                cols = slice(hd * GROUP, (hd + 1) * GROUP)
                p_lv[c, hd] = [lax.dot_general(qt[li][rows, cols], kt[li][rows, cols], NT_DIMS,
                                               preferred_element_type=F32)
                               for li in range(len(LEVELS))]
                upd[c, hd] = lax.dot_general(vb[rows, cols], kd[c][:, cols], TN_DIMS,
                                             preferred_element_type=F32)
            yield

        o_intra = {}
        for c in range(n_chunks):
            rows = slice(c * CHUNK, (c + 1) * CHUNK)
            for hd in range(heads):
                cols = slice(hd * GROUP, (hd + 1) * GROUP)
                a = jnp.zeros((CHUNK, CHUNK), F32)
                for li in range(len(LEVELS)):
                    a = a + jnp.where(masks[li], p_lv[c, hd][li], 0.0)
                o_intra[c, hd] = jnp.dot(a.astype(BF16), vb[rows, cols],
                                         preferred_element_type=F32)
            yield

        o_inter = {}
        for hd in range(heads):
            cols = slice(hd * GROUP, (hd + 1) * GROUP)
            st = st_ref[si, hd]
            for c in range(n_chunks):
                rows = slice(c * CHUNK, (c + 1) * CHUNK)
                o_inter[c, hd] = lax.dot_general(qd[rows, cols], st.astype(BF16), NT_DIMS,
                                                 preferred_element_type=F32)
                st = dec[c][:, cols] * st + upd[c, hd]
            st_ref[si, hd] = st
        yield

        o_heads = []
        for hd in range(heads):
            cols = slice(hd * GROUP, (hd + 1) * GROUP)
            diag = jnp.sum(qk[:, cols], axis=-1, keepdims=True)
            o_h = jnp.concatenate([o_intra[c, hd] + o_inter[c, hd] for c in range(n_chunks)],
                                  axis=0) + diag * v[:, cols]
            ms = jnp.mean(o_h * o_h, axis=-1, keepdims=True)
            g_h = g[:, cols]
            o_heads.append(o_h * lax.rsqrt(ms + RMS_EPS) * gn * (g_h * jax.nn.sigmoid(g_h)))
        y = jnp.concatenate([y_a] + o_heads, axis=-1).astype(BF16)
        yield

        mix = jnp.dot(y, w_out_ref[...], preferred_element_type=F32)
        yield
        out_ref[si] = _layer_norm(alpha * x + mix, g1_ref[...], b1_ref[...])

    _interleave([stream(si) for si in range(x_ref.shape[0])])


def _gelu_tanh(a):
    c = 0.7978845608028654
    return 0.5 * a * (1.0 + jnp.tanh(c * (a + 0.044715 * (a * a * a))))


def _ffn_kernel(h_ref, w_up_ref, conv_w_ref, conv_b_ref, w_down_ref, g2_ref, b2_ref,
                out_ref, halo_ref, pin_ref, pout_ref, gated_ref, *, tm, alpha, d_ff):
    ti = pl.program_id(1)
    r = tm // 8
    n_cb = h_ref.shape[2] // GROUP
    n_chunks = d_ff // FF_CHUNK
    chunks = [(slice(j * FF_CHUNK, (j + 1) * FF_CHUNK),
               slice(d_ff + j * FF_CHUNK, d_ff + (j + 1) * FF_CHUNK)) for j in range(n_chunks)]

    @pl.when(ti == 0)
    def _():
        halo_ref[...] = jnp.zeros_like(halo_ref)

    def stream(si):
        for c in range(n_cb):
            pin_ref[si, c] = h_ref[si, :, c * GROUP:(c + 1) * GROUP]
        h = jnp.concatenate(
            [jnp.concatenate([pin_ref[si, c, pl.ds(s, r, stride=8), :] for c in range(n_cb)],
                             axis=1) for s in range(8)], axis=0)
        hb = h.astype(BF16)

        def up(cols):
            return jnp.dot(hb, w_up_ref[:, cols], preferred_element_type=F32)

        def conv(u, cols):
            shifted = []
            for k in range(CONV_WIDTH - 1):
                blk = u[(6 + k) * r:(7 + k) * r, :]
                ext = jnp.concatenate([halo_ref[si, k, :, cols], blk], axis=0)
                halo_ref[si, k, :, cols] = blk[r - 8:, :]
                shifted.append(pltpu.roll(ext, 1, 0)[8:, :])
            u1 = jnp.concatenate([shifted[1], u[:7 * r, :]], axis=0)
            u2 = jnp.concatenate(shifted + [u[:6 * r, :]], axis=0)
            cw = conv_w_ref[:, cols]
            return conv_b_ref[:, cols] + cw[0:1, :] * u2 + cw[1:2, :] * u1 + cw[2:3, :] * u

        for ca, cv in chunks:
            gated_ref[si, :, ca] = (_gelu_tanh(conv(up(ca), ca)) * conv(up(cv), cv)).astype(BF16)
        yield
        acc = jnp.dot(gated_ref[si], w_down_ref[...], preferred_element_type=F32)
        res = _layer_norm(alpha * h + acc, g2_ref[...], b2_ref[...])
        for c in range(n_cb):
            for s in range(8):
                pout_ref[si, c, pl.ds(s, r, stride=8), :] = (
                    res[s * r:(s + 1) * r, c * GROUP:(c + 1) * GROUP])
        for c in range(n_cb):
            out_ref[si, :, c * GROUP:(c + 1) * GROUP] = pout_ref[si, c]

    _interleave([stream(si) for si in range(h_ref.shape[0])])


def _const_spec(shape):
    nd = len(shape)
    return pl.BlockSpec(shape, lambda b, t: (0,) * nd, pipeline_mode=pl.Buffered(1))


def _mixer_ln1(x, w_in, pool_w, pool_scale, lb, gnorm, w_out, g1, b1, *, tm, alpha):
    bsz, seq, d = x.shape
    heads = lb.shape[1] // GROUP
    kernel = functools.partial(_mixer_kernel, tm=tm, alpha=alpha, heads=heads)
    consts = (w_in, pool_w, pool_scale, lb, gnorm, w_out, g1, b1)
    return pl.pallas_call(
        kernel,
        grid=(bsz // STREAMS, seq // tm),
        in_specs=[pl.BlockSpec((STREAMS, tm, d), lambda b, t: (b, t, 0))]
        + [_const_spec(c.shape) for c in consts],
        out_specs=pl.BlockSpec((STREAMS, tm, d), lambda b, t: (b, t, 0)),
        out_shape=jax.ShapeDtypeStruct(x.shape, F32),
        scratch_shapes=[
            pltpu.VMEM((STREAMS, heads, GROUP, GROUP), F32),
            pltpu.VMEM((STREAMS, POOL_HALO, GROUP * len(POOL_WINDOWS)), F32),
        ],
        compiler_params=pltpu.CompilerParams(
            dimension_semantics=("arbitrary", "arbitrary"),
            vmem_limit_bytes=48 * 1024 * 1024),
        name="mixer_ln1",
    )(x, *consts)


def _ffn_ln2(h, w_up, conv_w, conv_b, w_down, g2, b2, *, tm, alpha):
    bsz, seq, d = h.shape
    d_ff = w_down.shape[0]
    kernel = functools.partial(_ffn_kernel, tm=tm, alpha=alpha, d_ff=d_ff)
    consts = (w_up, conv_w, conv_b, w_down, g2, b2)
    return pl.pallas_call(
        kernel,
        grid=(bsz // STREAMS, seq // tm),
        in_specs=[pl.BlockSpec((STREAMS, tm, d), lambda b, t: (b, t, 0))]
        + [_const_spec(c.shape) for c in consts],
        out_specs=pl.BlockSpec((STREAMS, tm, d), lambda b, t: (b, t, 0)),
        out_shape=jax.ShapeDtypeStruct(h.shape, F32),
        scratch_shapes=[
            pltpu.VMEM((STREAMS, CONV_WIDTH - 1, CONV_HALO, 2 * d_ff), F32),
            pltpu.VMEM((STREAMS, d // GROUP, tm, GROUP), F32),
            pltpu.VMEM((STREAMS, d // GROUP, tm, GROUP), F32),
            pltpu.VMEM((STREAMS, tm, d_ff), BF16),
        ],
        compiler_params=pltpu.CompilerParams(
            dimension_semantics=("arbitrary", "arbitrary"),
            vmem_limit_bytes=56 * 1024 * 1024),
        name="ffn_ln2",
    )(h, *consts)


def kernel(x, w_in, pool_w, pool_scale, hgrn_lb, hgrn_gnorm, w_out, ln1_g, ln1_b,
           w_up, conv_w, conv_b, w_down, ln2_g, ln2_b):
    depth = w_in.shape[0]
    assert depth == 1 and hgrn_lb.shape[0] == 2, "single-layer block expected"
    assert w_down.shape[1] % FF_CHUNK == 0 and x.shape[0] % STREAMS == 0
    alpha = (2.0 * depth) ** 0.25
    h = _mixer_ln1(
        x, w_in[0].astype(BF16), pool_w[0].astype(BF16), pool_scale, hgrn_lb,
        hgrn_gnorm, w_out[0].astype(BF16), ln1_g, ln1_b, tm=256, alpha=alpha)
    return _ffn_ln2(
        h, w_up[0].astype(BF16), conv_w[0], conv_b, w_down[0].astype(BF16),
        ln2_g, ln2_b, tm=256, alpha=alpha)
```

```python
import functools

import jax
import jax.numpy as jnp
from jax import lax
from jax.experimental import pallas as pl
from jax.experimental.pallas import tpu as pltpu

F32 = jnp.float32
BF16 = jnp.bfloat16

STREAMS = 2
CHUNK = 64
POOL_WINDOWS = (2, 4, 8, 16)
GROUP = 128
SUBLANES = 8
POOL_HALO = 16
CONV_WIDTH = 3
CONV_HALO = 8
FF_CHUNK = 256
LN_EPS = 1e-5
RMS_EPS = 1e-6
LEVELS = (32, 16, 8, 4, 2, 1)

NT_DIMS = (((1,), (1,)), ((), ()))
TN_DIMS = (((0,), (0,)), ((), ()))


def _interleave(streams):
    live = list(streams)
    while live:
        still = []
        for s in live:
            try:
                next(s)
                still.append(s)
            except StopIteration:
                pass
        live = still


def _layer_norm(z, g, b):
    mu = jnp.mean(z, axis=-1, keepdims=True)
    zc = z - mu
    var = jnp.mean(zc * zc, axis=-1, keepdims=True)
    return zc * lax.rsqrt(var + LN_EPS) * g + b


def _chunk_cumsum(a):
    n, c = a.shape
    a3 = a.reshape(n // SUBLANES, SUBLANES, c)
    j = lax.broadcasted_iota(jnp.int32, (1, SUBLANES, 1), 1)
    for k in (1, 2, 4):
        a3 = a3 + jnp.where(j >= k, pltpu.roll(a3, k, 1), 0.0)
    a4 = a3.reshape(n // CHUNK, CHUNK // SUBLANES, SUBLANES, c)
    carry = None
    groups = []
    for i in range(CHUNK // SUBLANES):
        g = a4[:, i] if carry is None else a4[:, i] + carry
        carry = g[:, SUBLANES - 1:SUBLANES, :]
        groups.append(g)
    return jnp.stack(groups, axis=1).reshape(n, c)


def _level_operands(b, fgt, qf, kk):
    n, c = b.shape
    out = []
    for half in LEVELS:
        if half >= SUBLANES:
            shape4 = (n // (2 * half), 2, half, c)
            b4, q4, k4 = b.reshape(shape4), qf.reshape(shape4), kk.reshape(shape4)
            ref = b4[:, 0, half - 1:half, :]
            z = jnp.stack([k4[:, 0] * jnp.exp(ref - b4[:, 0]),
                           q4[:, 1] * jnp.exp(b4[:, 1] - ref)], axis=1)
            out.append(z.reshape(n, c).astype(BF16))
    shape3 = (n // SUBLANES, SUBLANES, c)
    j = lax.broadcasted_iota(jnp.int32, (1, SUBLANES, 1), 1)
    b3, f3, q3, k3 = (a.reshape(shape3) for a in (b, fgt, qf, kk))
    sc = jnp.exp(-jnp.abs(b3 - b3[:, 3:4, :]))
    out.append((jnp.where(j >= 4, q3, k3) * sc).reshape(n, c).astype(BF16))
    qff = q3 * f3
    prev = pltpu.roll(f3, 1, 1)
    nxt = pltpu.roll(f3, SUBLANES - 1, 1)
    j4 = j & 3
    z2 = jnp.where(j4 >= 2, qff * jnp.where(j4 == 3, prev, 1.0),
                   k3 * jnp.where(j4 == 0, nxt, 1.0))
    out.append(z2.reshape(n, c).astype(BF16))
    out.append(jnp.where((j & 1) == 1, qff, k3).reshape(n, c).astype(BF16))
    return out


def _mixer_kernel(x_ref, w_in_ref, pool_w_ref, pool_scale_ref, lb_ref, gnorm_ref,
                  w_out_ref, g1_ref, b1_ref, out_ref, st_ref, halo_ref,
                  *, tm, alpha, heads):
    ti = pl.program_id(1)
    pw = GROUP * len(POOL_WINDOWS)
    hw = GROUP * heads
    n_chunks = tm // CHUNK

    @pl.when(ti == 0)
    def _():
        st_ref[...] = jnp.zeros_like(st_ref)
        halo_ref[...] = jnp.zeros_like(halo_ref)

    lbr = lb_ref[...]
    mx = jnp.max(lbr, axis=0, keepdims=True)
    e = jnp.exp(lbr - mx)
    lb = e[0:1, :] / jnp.sum(e, axis=0, keepdims=True)
    frames = (lax.broadcasted_iota(jnp.int32, (tm, 1), 0) + ti * tm + 1).astype(F32)
    inv_cnt = [1.0 / jnp.minimum(frames, float(w)) for w in POOL_WINDOWS]
    t_i = lax.broadcasted_iota(jnp.int32, (CHUNK, CHUNK), 0)
    s_i = lax.broadcasted_iota(jnp.int32, (CHUNK, CHUNK), 1)
    masks = [((t_i & half) != 0) & ((s_i & half) == 0)
             & ((t_i // (2 * half)) == (s_i // (2 * half))) for half in LEVELS]
    gn = gnorm_ref[...]

    def stream(si):
        x = x_ref[si]
        proj = jnp.dot(x.astype(BF16), w_in_ref[...], preferred_element_type=F32)
        xp = proj[:, :pw]
        q = proj[:, pw:pw + hw]
        f_pre = proj[:, pw + hw:pw + 2 * hw]
        v = proj[:, pw + 2 * hw:pw + 3 * hw]
        g = proj[:, pw + 3 * hw:pw + 4 * hw]
        yield

        ext = jnp.concatenate([halo_ref[si], xp], axis=0)
        halo_ref[si] = xp[tm - POOL_HALO:, :]
        ds = []
        for gi, w in enumerate(POOL_WINDOWS):
            cols = slice(gi * GROUP, (gi + 1) * GROUP)
            s = ext[:, cols]
            width = 1
            while width < w:
                s = s + pltpu.roll(s, width, 0)
                width *= 2
            ds.append((s[POOL_HALO:, :] * inv_cnt[gi] - xp[:, cols]).astype(BF16))
        fgt = lb + (1.0 - lb) * jax.nn.sigmoid(f_pre)
        lf = jnp.log(fgt)
        kk = 1.0 - fgt
        qf = q * jax.nn.sigmoid(q)
        b = _chunk_cumsum(lf)
        yield

        y_a = jnp.concatenate(
            [jnp.dot(ds[gi], pool_w_ref[gi], preferred_element_type=F32)
             for gi in range(len(POOL_WINDOWS))], axis=-1) * pool_scale_ref[...]
        yield

        z = _level_operands(b, fgt, qf, kk)
        qk = qf * kk
        qd = (qf * jnp.exp(b)).astype(BF16)
        vb = v.astype(BF16)
        kd, dec = [], []
        for c in range(n_chunks):
            b_c = b[c * CHUNK:(c + 1) * CHUNK, :]
            b_last = b_c[CHUNK - 1:CHUNK, :]
            kd.append((kk[c * CHUNK:(c + 1) * CHUNK, :] * jnp.exp(b_last - b_c)).astype(BF16))
            dec.append(jnp.exp(b_last))
        yield

        p_lv = {}
        upd = {}
        for c in range(n_chunks):
            rows = slice(c * CHUNK, (c + 1) * CHUNK)
            for hd in range(heads):
                cols = slice(hd * GROUP, (hd + 1) * GROUP)
                p_lv[c, hd] = [lax.dot_general(z[li][rows, cols], z[li][rows, cols], NT_DIMS,
                                               preferred_element_type=F32)
                               for li in range(len(LEVELS))]
                upd[c, hd] = lax.dot_general(vb[rows, cols], kd[c][:, cols], TN_DIMS,
                                             preferred_element_type=F32)
            yield

        o_intra = {}
        for c in range(n_chunks):
            rows = slice(c * CHUNK, (c + 1) * CHUNK)
            for hd in range(heads):
                cols = slice(hd * GROUP, (hd + 1) * GROUP)
                a = jnp.zeros((CHUNK, CHUNK), F32)
                for li in range(len(LEVELS)):
                    a = jnp.where(masks[li], p_lv[c, hd][li], a)
                o_intra[c, hd] = jnp.dot(a.astype(BF16), vb[rows, cols],
                                         preferred_element_type=F32)
            yield

        o_inter = {}
        for hd in range(heads):
            cols = slice(hd * GROUP, (hd + 1) * GROUP)
            st = st_ref[si, hd]
            for c in range(n_chunks):
                rows = slice(c * CHUNK, (c + 1) * CHUNK)
                o_inter[c, hd] = lax.dot_general(qd[rows, cols], st.astype(BF16), NT_DIMS,
                                                 preferred_element_type=F32)
                st = dec[c][:, cols] * st + upd[c, hd]
            st_ref[si, hd] = st
        yield

        o_heads = []
        for hd in range(heads):
            cols = slice(hd * GROUP, (hd + 1) * GROUP)
            diag = jnp.sum(qk[:, cols], axis=-1, keepdims=True)
            o_h = jnp.concatenate([o_intra[c, hd] + o_inter[c, hd] for c in range(n_chunks)],
                                  axis=0) + diag * v[:, cols]
            ms = jnp.mean(o_h * o_h, axis=-1, keepdims=True)
            g_h = g[:, cols]
            o_heads.append(o_h * lax.rsqrt(ms + RMS_EPS) * gn * (g_h * jax.nn.sigmoid(g_h)))
        y = jnp.concatenate([y_a] + o_heads, axis=-1).astype(BF16)
        yield

        mix = jnp.dot(y, w_out_ref[...], preferred_element_type=F32)
        yield
        out_ref[si] = _layer_norm(alpha * x + mix, g1_ref[...], b1_ref[...])

    _interleave([stream(si) for si in range(x_ref.shape[0])])


def _gelu_tanh(a):
    c = 0.7978845608028654
    return 0.5 * a * (1.0 + jnp.tanh(c * (a + 0.044715 * (a * a * a))))


def _ffn_kernel(h_ref, w_up_ref, conv_w_ref, conv_b_ref, w_down_ref, g2_ref, b2_ref,
                out_ref, halo_ref, pin_ref, pout_ref, gated_ref, *, tm, alpha, d_ff):
    ti = pl.program_id(1)
    r = tm // SUBLANES
    n_cb = h_ref.shape[2] // GROUP
    n_chunks = d_ff // FF_CHUNK
    chunks = [(slice(j * FF_CHUNK, (j + 1) * FF_CHUNK),
               slice(d_ff + j * FF_CHUNK, d_ff + (j + 1) * FF_CHUNK)) for j in range(n_chunks)]

    @pl.when(ti == 0)
    def _():
        halo_ref[...] = jnp.zeros_like(halo_ref)

    def stream(si):
        for c in range(n_cb):
            pin_ref[si, c] = h_ref[si, :, c * GROUP:(c + 1) * GROUP]
        h = jnp.concatenate(
            [jnp.concatenate([pin_ref[si, c, pl.ds(s, r, stride=SUBLANES), :]
                              for c in range(n_cb)], axis=1) for s in range(SUBLANES)], axis=0)
        hb = h.astype(BF16)

        def up(cols):
            return jnp.dot(hb, w_up_ref[:, cols], preferred_element_type=F32)

        def conv(u, cols):
            shifted = []
            for k in range(CONV_WIDTH - 1):
                blk = u[(6 + k) * r:(7 + k) * r, :]
                ext = jnp.concatenate([halo_ref[si, k, :, cols], blk], axis=0)
                halo_ref[si, k, :, cols] = blk[r - CONV_HALO:, :]
                shifted.append(pltpu.roll(ext, 1, 0)[CONV_HALO:, :])
            u1 = jnp.concatenate([shifted[1], u[:7 * r, :]], axis=0)
            u2 = jnp.concatenate(shifted + [u[:6 * r, :]], axis=0)
            cw = conv_w_ref[:, cols]
            return conv_b_ref[:, cols] + cw[0:1, :] * u2 + cw[1:2, :] * u1 + cw[2:3, :] * u

        for ca, cv in chunks:
            gated_ref[si, :, ca] = (_gelu_tanh(conv(up(ca), ca)) * conv(up(cv), cv)).astype(BF16)
        yield
        acc = jnp.dot(gated_ref[si], w_down_ref[...], preferred_element_type=F32)
        res = _layer_norm(alpha * h + acc, g2_ref[...], b2_ref[...])
        for c in range(n_cb):
            for s in range(SUBLANES):
                pout_ref[si, c, pl.ds(s, r, stride=SUBLANES), :] = (
                    res[s * r:(s + 1) * r, c * GROUP:(c + 1) * GROUP])
        for c in range(n_cb):
            out_ref[si, :, c * GROUP:(c + 1) * GROUP] = pout_ref[si, c]

    _interleave([stream(si) for si in range(h_ref.shape[0])])


def _const_spec(shape):
    nd = len(shape)
    return pl.BlockSpec(shape, lambda b, t: (0,) * nd, pipeline_mode=pl.Buffered(1))


def _mixer_ln1(x, w_in, pool_w, pool_scale, lb, gnorm, w_out, g1, b1, *, tm, alpha):
    bsz, seq, d = x.shape
    heads = lb.shape[1] // GROUP
    kernel = functools.partial(_mixer_kernel, tm=tm, alpha=alpha, heads=heads)
    consts = (w_in, pool_w, pool_scale, lb, gnorm, w_out, g1, b1)
    return pl.pallas_call(
        kernel,
        grid=(bsz // STREAMS, seq // tm),
        in_specs=[pl.BlockSpec((STREAMS, tm, d), lambda b, t: (b, t, 0))]
        + [_const_spec(c.shape) for c in consts],
        out_specs=pl.BlockSpec((STREAMS, tm, d), lambda b, t: (b, t, 0)),
        out_shape=jax.ShapeDtypeStruct(x.shape, F32),
        scratch_shapes=[
            pltpu.VMEM((STREAMS, heads, GROUP, GROUP), F32),
            pltpu.VMEM((STREAMS, POOL_HALO, GROUP * len(POOL_WINDOWS)), F32),
        ],
        compiler_params=pltpu.CompilerParams(
            dimension_semantics=("arbitrary", "arbitrary"),
            vmem_limit_bytes=48 * 1024 * 1024),
        name="mixer_ln1",
    )(x, *consts)


def _ffn_ln2(h, w_up, conv_w, conv_b, w_down, g2, b2, *, tm, alpha):
    bsz, seq, d = h.shape
    d_ff = w_down.shape[0]
    kernel = functools.partial(_ffn_kernel, tm=tm, alpha=alpha, d_ff=d_ff)
    consts = (w_up, conv_w, conv_b, w_down, g2, b2)
    return pl.pallas_call(
        kernel,
        grid=(bsz // STREAMS, seq // tm),
        in_specs=[pl.BlockSpec((STREAMS, tm, d), lambda b, t: (b, t, 0))]
        + [_const_spec(c.shape) for c in consts],
        out_specs=pl.BlockSpec((STREAMS, tm, d), lambda b, t: (b, t, 0)),
        out_shape=jax.ShapeDtypeStruct(h.shape, F32),
        scratch_shapes=[
            pltpu.VMEM((STREAMS, CONV_WIDTH - 1, CONV_HALO, 2 * d_ff), F32),
            pltpu.VMEM((STREAMS, d // GROUP, tm, GROUP), F32),
            pltpu.VMEM((STREAMS, d // GROUP, tm, GROUP), F32),
            pltpu.VMEM((STREAMS, tm, d_ff), BF16),
        ],
        compiler_params=pltpu.CompilerParams(
            dimension_semantics=("arbitrary", "arbitrary"),
            vmem_limit_bytes=56 * 1024 * 1024),
        name="ffn_ln2",
    )(h, *consts)


def kernel(x, w_in, pool_w, pool_scale, hgrn_lb, hgrn_gnorm, w_out, ln1_g, ln1_b,
           w_up, conv_w, conv_b, w_down, ln2_g, ln2_b):
    depth = w_in.shape[0]
    assert depth == 1 and hgrn_lb.shape[0] == 2, "single-layer block expected"
    assert w_down.shape[1] % FF_CHUNK == 0 and x.shape[0] % STREAMS == 0
    alpha = (2.0 * depth) ** 0.25
    h = _mixer_ln1(
        x, w_in[0].astype(BF16), pool_w[0].astype(BF16), pool_scale, hgrn_lb,
        hgrn_gnorm, w_out[0].astype(BF16), ln1_g, ln1_b, tm=256, alpha=alpha)
    return _ffn_ln2(
        h, w_up[0].astype(BF16), conv_w[0], conv_b, w_down[0].astype(BF16),
        ln2_g, ln2_b, tm=256, alpha=alpha)
```

```python
import functools
import math

import jax
import jax.numpy as jnp
from jax import lax
from jax.experimental import pallas as pl
from jax.experimental.pallas import tpu as pltpu

F32 = jnp.float32
BF16 = jnp.bfloat16

STREAMS = 2
FFN_STREAMS = 2
CHUNK = 64
POOL_WINDOWS = (2, 4, 8, 16)
GROUP = 128
SUBLANES = 8
POOL_HALO = 16
CONV_WIDTH = 3
CONV_HALO = 8
FF_CHUNK = 256
LN_EPS = 1e-5
RMS_EPS = 1e-6
LEVELS = (32, 16, 8, 4, 2, 1)
LOCAL = 16
MAX_LOCAL_EXPONENT = 80.0
MIN_FORGET_LOCAL = math.exp(-MAX_LOCAL_EXPONENT / LOCAL)

NT_DIMS = (((1,), (1,)), ((), ()))
TN_DIMS = (((0,), (0,)), ((), ()))


def _advance(gens):
    while gens:
        g = gens.pop(0)
        try:
            next(g)
        except StopIteration:
            continue
        gens.append(g)
        return True
    return False


def _interleave(streams, fill=(), plan=None):
    live, fill = list(streams), list(fill)
    plan = plan or {}
    n = 0
    while True:
        for _ in range(plan.get(n, 0)):
            _advance(fill)
        if not _advance(live):
            break
        n += 1
    while _advance(fill):
        pass


def _layer_norm(z, g, b):
    mu = jnp.mean(z, axis=-1, keepdims=True)
    zc = z - mu
    var = jnp.mean(zc * zc, axis=-1, keepdims=True)
    return zc * lax.rsqrt(var + LN_EPS) * g + b


def _chunk_cumsum(a):
    n, c = a.shape
    a3 = a.reshape(n // SUBLANES, SUBLANES, c)
    j = lax.broadcasted_iota(jnp.int32, (1, SUBLANES, 1), 1)
    for k in (1, 2, 4):
        a3 = a3 + jnp.where(j >= k, pltpu.roll(a3, k, 1), 0.0)
    a4 = a3.reshape(n // CHUNK, CHUNK // SUBLANES, SUBLANES, c)
    carry = None
    groups = []
    for i in range(CHUNK // SUBLANES):
        g = a4[:, i] if carry is None else a4[:, i] + carry
        carry = g[:, SUBLANES - 1:SUBLANES, :]
        groups.append(g)
    return jnp.stack(groups, axis=1).reshape(n, c)


def _local_operands(b, qf, kk):
    n, c = b.shape
    per_chunk = CHUNK // LOCAL
    b4 = b.reshape(n // CHUNK, per_chunk, LOCAL, c)
    last = b4[:, :per_chunk - 1, LOCAL - 1:LOCAL, :]
    ref = jnp.concatenate([jnp.zeros_like(last[:, :1]), last], axis=1)
    d = (b4 - ref).reshape(n, c)
    return (qf * jnp.exp(d)).astype(BF16), (kk * jnp.exp(-d)).astype(BF16)


def _level_operands(b, fgt, qf, kk, min_half=1):
    n, c = b.shape
    out = []
    for half in LEVELS:
        if half >= min_half and half >= SUBLANES:
            shape4 = (n // (2 * half), 2, half, c)
            b4, q4, k4 = b.reshape(shape4), qf.reshape(shape4), kk.reshape(shape4)
            ref = b4[:, 0, half - 1:half, :]
            z = jnp.stack([k4[:, 0] * jnp.exp(ref - b4[:, 0]),
                           q4[:, 1] * jnp.exp(b4[:, 1] - ref)], axis=1)
            out.append(z.reshape(n, c).astype(BF16))
    if min_half >= SUBLANES:
        return out
    shape3 = (n // SUBLANES, SUBLANES, c)
    j = lax.broadcasted_iota(jnp.int32, (1, SUBLANES, 1), 1)
    b3, f3, q3, k3 = (a.reshape(shape3) for a in (b, fgt, qf, kk))
    sc = jnp.exp(-jnp.abs(b3 - b3[:, 3:4, :]))
    out.append((jnp.where(j >= 4, q3, k3) * sc).reshape(n, c).astype(BF16))
    qff = q3 * f3
    prev = pltpu.roll(f3, 1, 1)
    nxt = pltpu.roll(f3, SUBLANES - 1, 1)
    j4 = j & 3
    z2 = jnp.where(j4 >= 2, qff * jnp.where(j4 == 3, prev, 1.0),
                   k3 * jnp.where(j4 == 0, nxt, 1.0))
    out.append(z2.reshape(n, c).astype(BF16))
    out.append(jnp.where((j & 1) == 1, qff, k3).reshape(n, c).astype(BF16))
    return out


def _mixer_kernel(x_ref, w_in_ref, pool_w_ref, pool_scale_ref, lb_ref, gnorm_ref,
                  w_out_ref, g1_ref, b1_ref, out_ref, fpre_ref, st_ref, halo_ref,
                  *, tm, alpha, heads):
    ti = pl.program_id(1)
    pw = GROUP * len(POOL_WINDOWS)
    hw = GROUP * heads
    n_chunks = tm // CHUNK

    @pl.when(ti == 0)
    def _():
        st_ref[...] = jnp.zeros_like(st_ref)
        halo_ref[...] = jnp.zeros_like(halo_ref)

    lbr = lb_ref[...]
    mx = jnp.max(lbr, axis=0, keepdims=True)
    e = jnp.exp(lbr - mx)
    lb = e[0:1, :] / jnp.sum(e, axis=0, keepdims=True)
    frames = (lax.broadcasted_iota(jnp.int32, (tm, 1), 0) + ti * tm + 1).astype(F32)
    inv_cnt = [1.0 / jnp.minimum(frames, float(w)) for w in POOL_WINDOWS]
    t_i = lax.broadcasted_iota(jnp.int32, (CHUNK, CHUNK), 0)
    s_i = lax.broadcasted_iota(jnp.int32, (CHUNK, CHUNK), 1)
    masks = [((t_i & half) != 0) & ((s_i & half) == 0)
             & ((t_i // (2 * half)) == (s_i // (2 * half))) for half in LEVELS]
    local_mask = ((t_i // LOCAL) == (s_i // LOCAL)) & (s_i <= t_i)
    gn = gnorm_ref[...]

    f_cols = slice(pw + hw, pw + 2 * hw)
    f_pre_min = None
    for si in range(x_ref.shape[0]):
        fpre_ref[si] = jnp.dot(x_ref[si].astype(BF16), w_in_ref[:, f_cols],
                               preferred_element_type=F32)
        m = jnp.min(fpre_ref[si], axis=0, keepdims=True)
        f_pre_min = m if f_pre_min is None else jnp.minimum(f_pre_min, m)
    fgt_min = jnp.min(lb + (1.0 - lb) * jax.nn.sigmoid(f_pre_min))
    local_ok = fgt_min >= MIN_FORGET_LOCAL

    def stream(si, fast):
        x = x_ref[si]
        xb = x.astype(BF16)
        left = jnp.dot(xb, w_in_ref[:, 0:pw + hw], preferred_element_type=F32)
        right = jnp.dot(xb, w_in_ref[:, pw + 2 * hw:pw + 4 * hw], preferred_element_type=F32)
        xp = left[:, 0:pw]
        q = left[:, pw:pw + hw]
        f_pre = fpre_ref[si]
        v = right[:, 0:hw]
        g = right[:, hw:2 * hw]
        yield

        ext = jnp.concatenate([halo_ref[si], xp], axis=0)
        halo_ref[si] = xp[tm - POOL_HALO:, :]
        ds = []
        for gi, w in enumerate(POOL_WINDOWS):
            cols = slice(gi * GROUP, (gi + 1) * GROUP)
            s = ext[:, cols]
            width = 1
            while width < w:
                s = s + pltpu.roll(s, width, 0)
                width *= 2
            ds.append((s[POOL_HALO:, :] * inv_cnt[gi] - xp[:, cols]).astype(BF16))
        fgt = lb + (1.0 - lb) * jax.nn.sigmoid(f_pre)
        lf = jnp.log(fgt)
        kk = 1.0 - fgt
        qf = q * jax.nn.sigmoid(q)
        b = _chunk_cumsum(lf)
        yield

        y_a = jnp.concatenate(
            [jnp.dot(ds[gi], pool_w_ref[gi], preferred_element_type=F32)
             for gi in range(len(POOL_WINDOWS))], axis=-1) * pool_scale_ref[...]
        yield

        if fast:
            z = _level_operands(b, fgt, qf, kk, min_half=LOCAL)
            terms = [(zl, zl, masks[li]) for li, zl in enumerate(z)]
            terms.append(_local_operands(b, qf, kk) + (local_mask,))
        else:
            z = _level_operands(b, fgt, qf, kk)
            terms = [(zl, zl, masks[li]) for li, zl in enumerate(z)]
            qk = qf * kk
        qd = (qf * jnp.exp(b)).astype(BF16)
        vb = v.astype(BF16)
        kd, dec = [], []
        for c in range(n_chunks):
            b_c = b[c * CHUNK:(c + 1) * CHUNK, :]
            b_last = b_c[CHUNK - 1:CHUNK, :]
            kd.append((kk[c * CHUNK:(c + 1) * CHUNK, :] * jnp.exp(b_last - b_c)).astype(BF16))
            dec.append(jnp.exp(b_last))
        yield

        p_lv = {}
        upd = {}
        for c in range(n_chunks):
            rows = slice(c * CHUNK, (c + 1) * CHUNK)
            for hd in range(heads):
                cols = slice(hd * GROUP, (hd + 1) * GROUP)
                p_lv[c, hd] = [lax.dot_general(lhs[rows, cols], rhs[rows, cols], NT_DIMS,
                                               preferred_element_type=F32)
                               for lhs, rhs, _ in terms]
                upd[c, hd] = lax.dot_general(vb[rows, cols], kd[c][:, cols], TN_DIMS,
                                             preferred_element_type=F32)
            yield

        o_intra = {}
        for c in range(n_chunks):
            rows = slice(c * CHUNK, (c + 1) * CHUNK)
            for hd in range(heads):
                cols = slice(hd * GROUP, (hd + 1) * GROUP)
                a = jnp.zeros((CHUNK, CHUNK), F32)
                for (_, _, mask), p in zip(terms, p_lv[c, hd]):
                    a = jnp.where(mask, p, a)
                o_intra[c, hd] = jnp.dot(a.astype(BF16), vb[rows, cols],
                                         preferred_element_type=F32)
            yield

        o_inter = {}
        for hd in range(heads):
            cols = slice(hd * GROUP, (hd + 1) * GROUP)
            st = st_ref[si, hd]
            for c in range(n_chunks):
                rows = slice(c * CHUNK, (c + 1) * CHUNK)
                o_inter[c, hd] = lax.dot_general(qd[rows, cols], st.astype(BF16), NT_DIMS,
                                                 preferred_element_type=F32)
                st = dec[c][:, cols] * st + upd[c, hd]
            st_ref[si, hd] = st
        yield

        o_heads = []
        for hd in range(heads):
            cols = slice(hd * GROUP, (hd + 1) * GROUP)
            o_h = jnp.concatenate([o_intra[c, hd] + o_inter[c, hd] for c in range(n_chunks)],
                                  axis=0)
            if not fast:
                diag = jnp.sum(qk[:, cols], axis=-1, keepdims=True)
                o_h = o_h + diag * v[:, cols]
            ms = jnp.mean(o_h * o_h, axis=-1, keepdims=True)
            g_h = g[:, cols]
            o_heads.append(o_h * lax.rsqrt(ms + RMS_EPS) * gn * (g_h * jax.nn.sigmoid(g_h)))
        y = jnp.concatenate([y_a] + o_heads, axis=-1).astype(BF16)
        yield

        mix = jnp.dot(y, w_out_ref[...], preferred_element_type=F32)
        yield
        out_ref[si] = _layer_norm(alpha * x + mix, g1_ref[...], b1_ref[...])

    def body(fast):
        _interleave([stream(si, fast) for si in range(x_ref.shape[0])])

    pl.when(local_ok)(functools.partial(body, True))
    pl.when(jnp.logical_not(local_ok))(functools.partial(body, False))


def _gelu_tanh(a):
    c = 0.7978845608028654
    return 0.5 * a * (1.0 + jnp.tanh(c * (a + 0.044715 * (a * a * a))))


def _ffn_kernel(h_ref, w_up_ref, conv_w_ref, conv_b_ref, w_down_ref, g2_ref, b2_ref,
                out_ref, halo_ref, pin_ref, pout_ref, gated_ref, *, tm, alpha, d_ff):
    ti = pl.program_id(1)
    r = tm // SUBLANES
    n_cb = h_ref.shape[2] // GROUP
    n_chunks = d_ff // FF_CHUNK
    chunks = [(slice(j * FF_CHUNK, (j + 1) * FF_CHUNK),
               slice(d_ff + j * FF_CHUNK, d_ff + (j + 1) * FF_CHUNK)) for j in range(n_chunks)]

    @pl.when(ti == 0)
    def _():
        halo_ref[...] = jnp.zeros_like(halo_ref)

    def stream(si):
        for c in range(n_cb):
            pin_ref[si, c] = h_ref[si, :, c * GROUP:(c + 1) * GROUP]
        h = jnp.concatenate(
            [jnp.concatenate([pin_ref[si, c, pl.ds(s, r, stride=SUBLANES), :]
                              for c in range(n_cb)], axis=1) for s in range(SUBLANES)], axis=0)
        hb = h.astype(BF16)

        def up(cols):
            return jnp.dot(hb, w_up_ref[:, cols], preferred_element_type=F32)

        def conv(u, cols):
            shifted = []
            for k in range(CONV_WIDTH - 1):
                blk = u[(6 + k) * r:(7 + k) * r, :]
                ext = jnp.concatenate([halo_ref[si, k, :, cols], blk], axis=0)
                halo_ref[si, k, :, cols] = blk[r - CONV_HALO:, :]
                shifted.append(pltpu.roll(ext, 1, 0)[CONV_HALO:, :])
            u1 = jnp.concatenate([shifted[1], u[:7 * r, :]], axis=0)
            u2 = jnp.concatenate(shifted + [u[:6 * r, :]], axis=0)
            cw = conv_w_ref[:, cols]
            return conv_b_ref[:, cols] + cw[0:1, :] * u2 + cw[1:2, :] * u1 + cw[2:3, :] * u

        for ca, cv in chunks:
            gated_ref[si, :, ca] = (_gelu_tanh(conv(up(ca), ca)) * conv(up(cv), cv)).astype(BF16)
        yield
        acc = jnp.dot(gated_ref[si], w_down_ref[...], preferred_element_type=F32)
        res = _layer_norm(alpha * h + acc, g2_ref[...], b2_ref[...])
        for c in range(n_cb):
            for s in range(SUBLANES):
                pout_ref[si, c, pl.ds(s, r, stride=SUBLANES), :] = (
                    res[s * r:(s + 1) * r, c * GROUP:(c + 1) * GROUP])
        for c in range(n_cb):
            out_ref[si, :, c * GROUP:(c + 1) * GROUP] = pout_ref[si, c]

    _interleave([stream(si) for si in range(h_ref.shape[0])])


def _const_spec(shape):
    nd = len(shape)
    return pl.BlockSpec(shape, lambda b, t: (0,) * nd, pipeline_mode=pl.Buffered(1))


def _mixer_ln1(x, w_in, pool_w, pool_scale, lb, gnorm, w_out, g1, b1, *, tm, alpha):
    bsz, seq, d = x.shape
    heads = lb.shape[1] // GROUP
    kernel = functools.partial(_mixer_kernel, tm=tm, alpha=alpha, heads=heads)
    consts = (w_in, pool_w, pool_scale, lb, gnorm, w_out, g1, b1)
    return pl.pallas_call(
        kernel,
        grid=(bsz // STREAMS, seq // tm),
        in_specs=[pl.BlockSpec((STREAMS, tm, d), lambda b, t: (b, t, 0))]
        + [_const_spec(c.shape) for c in consts],
        out_specs=pl.BlockSpec((STREAMS, tm, d), lambda b, t: (b, t, 0)),
        out_shape=jax.ShapeDtypeStruct(x.shape, F32),
        scratch_shapes=[
            pltpu.VMEM((STREAMS, tm, lb.shape[1]), F32),
            pltpu.VMEM((STREAMS, heads, GROUP, GROUP), F32),
            pltpu.VMEM((STREAMS, POOL_HALO, GROUP * len(POOL_WINDOWS)), F32),
        ],
        compiler_params=pltpu.CompilerParams(
            dimension_semantics=("arbitrary", "arbitrary"),
            vmem_limit_bytes=48 * 1024 * 1024),
        name="mixer_ln1",
    )(x, *consts)


def _ffn_ln2(h, w_up, conv_w, conv_b, w_down, g2, b2, *, tm, alpha):
    bsz, seq, d = h.shape
    d_ff = w_down.shape[0]
    kernel = functools.partial(_ffn_kernel, tm=tm, alpha=alpha, d_ff=d_ff)
    consts = (w_up, conv_w, conv_b, w_down, g2, b2)
    ns = FFN_STREAMS
    return pl.pallas_call(
        kernel,
        grid=(bsz // ns, seq // tm),
        in_specs=[pl.BlockSpec((ns, tm, d), lambda b, t: (b, t, 0))]
        + [_const_spec(c.shape) for c in consts],
        out_specs=pl.BlockSpec((ns, tm, d), lambda b, t: (b, t, 0)),
        out_shape=jax.ShapeDtypeStruct(h.shape, F32),
        scratch_shapes=[
            pltpu.VMEM((ns, CONV_WIDTH - 1, CONV_HALO, 2 * d_ff), F32),
            pltpu.VMEM((ns, d // GROUP, tm, GROUP), F32),
            pltpu.VMEM((ns, d // GROUP, tm, GROUP), F32),
            pltpu.VMEM((ns, tm, d_ff), BF16),
        ],
        compiler_params=pltpu.CompilerParams(
            dimension_semantics=("arbitrary", "arbitrary"),
            vmem_limit_bytes=56 * 1024 * 1024),
        name="ffn_ln2",
    )(h, *consts)


def kernel(x, w_in, pool_w, pool_scale, hgrn_lb, hgrn_gnorm, w_out, ln1_g, ln1_b,
           w_up, conv_w, conv_b, w_down, ln2_g, ln2_b):
    depth = w_in.shape[0]
    assert depth == 1 and hgrn_lb.shape[0] == 2, "single-layer block expected"
    assert w_down.shape[1] % FF_CHUNK == 0
    assert x.shape[0] % STREAMS == 0 and x.shape[0] % FFN_STREAMS == 0
    alpha = (2.0 * depth) ** 0.25
    h = _mixer_ln1(
        x, w_in[0].astype(BF16), pool_w[0].astype(BF16), pool_scale, hgrn_lb,
        hgrn_gnorm, w_out[0].astype(BF16), ln1_g, ln1_b, tm=256, alpha=alpha)
    return _ffn_ln2(
        h, w_up[0].astype(BF16), conv_w[0], conv_b, w_down[0].astype(BF16),
        ln2_g, ln2_b, tm=256, alpha=alpha)
```

```python
import functools
import math

import jax
import jax.numpy as jnp
from jax import lax
from jax.experimental import pallas as pl
from jax.experimental.pallas import tpu as pltpu

F32 = jnp.float32
BF16 = jnp.bfloat16

STREAMS = 2
FFN_STREAMS = 2
CHUNK = 64
POOL_WINDOWS = (2, 4, 8, 16)
GROUP = 128
SUBLANES = 8
POOL_HALO = 16
CONV_WIDTH = 3
CONV_HALO = 8
FF_CHUNK = 256
LN_EPS = 1e-5
RMS_EPS = 1e-6
LEVELS = (32, 16, 8, 4, 2, 1)
LOCAL = 16
MAX_LOCAL_EXPONENT = 80.0
MIN_FORGET_LOCAL = math.exp(-MAX_LOCAL_EXPONENT / LOCAL)
GATES_AHEAD_PLAN = {4: 1, 6: 1}

NT_DIMS = (((1,), (1,)), ((), ()))
TN_DIMS = (((0,), (0,)), ((), ()))


def _advance(gens):
    while gens:
        g = gens.pop(0)
        try:
            next(g)
        except StopIteration:
            continue
        gens.append(g)
        return True
    return False


def _interleave(streams, fill=(), plan=None):
    live, fill = list(streams), list(fill)
    plan = plan or {}
    n = 0
    while True:
        for _ in range(plan.get(n, 0)):
            _advance(fill)
        if not _advance(live):
            break
        n += 1
    while _advance(fill):
        pass


def _layer_norm(z, g, b):
    mu = jnp.mean(z, axis=-1, keepdims=True)
    zc = z - mu
    var = jnp.mean(zc * zc, axis=-1, keepdims=True)
    return zc * lax.rsqrt(var + LN_EPS) * g + b


def _chunk_cumsum(a):
    n, c = a.shape
    a3 = a.reshape(n // SUBLANES, SUBLANES, c)
    j = lax.broadcasted_iota(jnp.int32, (1, SUBLANES, 1), 1)
    for k in (1, 2, 4):
        a3 = a3 + jnp.where(j >= k, pltpu.roll(a3, k, 1), 0.0)
    a4 = a3.reshape(n // CHUNK, CHUNK // SUBLANES, SUBLANES, c)
    carry = None
    groups = []
    for i in range(CHUNK // SUBLANES):
        g = a4[:, i] if carry is None else a4[:, i] + carry
        carry = g[:, SUBLANES - 1:SUBLANES, :]
        groups.append(g)
    return jnp.stack(groups, axis=1).reshape(n, c)


def _local_operands(b, qf, kk):
    n, c = b.shape
    per_chunk = CHUNK // LOCAL
    b4 = b.reshape(n // CHUNK, per_chunk, LOCAL, c)
    last = b4[:, :per_chunk - 1, LOCAL - 1:LOCAL, :]
    ref = jnp.concatenate([jnp.zeros_like(last[:, :1]), last], axis=1)
    d = (b4 - ref).reshape(n, c)
    return (qf * jnp.exp(d)).astype(BF16), (kk * jnp.exp(-d)).astype(BF16)


def _level_operands(b, fgt, qf, kk, min_half=1):
    n, c = b.shape
    out = []
    for half in LEVELS:
        if half >= min_half and half >= SUBLANES:
            shape4 = (n // (2 * half), 2, half, c)
            b4, q4, k4 = b.reshape(shape4), qf.reshape(shape4), kk.reshape(shape4)
            ref = b4[:, 0, half - 1:half, :]
            z = jnp.stack([k4[:, 0] * jnp.exp(ref - b4[:, 0]),
                           q4[:, 1] * jnp.exp(b4[:, 1] - ref)], axis=1)
            out.append(z.reshape(n, c).astype(BF16))
    if min_half >= SUBLANES:
        return out
    shape3 = (n // SUBLANES, SUBLANES, c)
    j = lax.broadcasted_iota(jnp.int32, (1, SUBLANES, 1), 1)
    b3, f3, q3, k3 = (a.reshape(shape3) for a in (b, fgt, qf, kk))
    sc = jnp.exp(-jnp.abs(b3 - b3[:, 3:4, :]))
    out.append((jnp.where(j >= 4, q3, k3) * sc).reshape(n, c).astype(BF16))
    qff = q3 * f3
    prev = pltpu.roll(f3, 1, 1)
    nxt = pltpu.roll(f3, SUBLANES - 1, 1)
    j4 = j & 3
    z2 = jnp.where(j4 >= 2, qff * jnp.where(j4 == 3, prev, 1.0),
                   k3 * jnp.where(j4 == 0, nxt, 1.0))
    out.append(z2.reshape(n, c).astype(BF16))
    out.append(jnp.where((j & 1) == 1, qff, k3).reshape(n, c).astype(BF16))
    return out


def _mixer_kernel(x_ref, xn_ref, w_in_ref, pool_w_ref, pool_scale_ref, lb_ref, gnorm_ref,
                  w_out_ref, g1_ref, b1_ref, out_ref, fpre_ref, flag_ref, st_ref, halo_ref,
                  *, tm, alpha, heads):
    ti = pl.program_id(1)
    pw = GROUP * len(POOL_WINDOWS)
    hw = GROUP * heads
    n_chunks = tm // CHUNK

    @pl.when(ti == 0)
    def _():
        st_ref[...] = jnp.zeros_like(st_ref)
        halo_ref[...] = jnp.zeros_like(halo_ref)

    lbr = lb_ref[...]
    mx = jnp.max(lbr, axis=0, keepdims=True)
    e = jnp.exp(lbr - mx)
    lb = e[0:1, :] / jnp.sum(e, axis=0, keepdims=True)
    frames = (lax.broadcasted_iota(jnp.int32, (tm, 1), 0) + ti * tm + 1).astype(F32)
    inv_cnt = [1.0 / jnp.minimum(frames, float(w)) for w in POOL_WINDOWS]
    t_i = lax.broadcasted_iota(jnp.int32, (CHUNK, CHUNK), 0)
    s_i = lax.broadcasted_iota(jnp.int32, (CHUNK, CHUNK), 1)
    masks = [((t_i & half) != 0) & ((s_i & half) == 0)
             & ((t_i // (2 * half)) == (s_i // (2 * half))) for half in LEVELS]
    local_mask = ((t_i // LOCAL) == (s_i // LOCAL)) & (s_i <= t_i)
    gn = gnorm_ref[...]

    f_cols = slice(pw + hw, pw + 2 * hw)
    n_s = x_ref.shape[0]

    def project_gates(src_ref):
        f_pre_min = None
        for si in range(n_s):
            fpre_ref[si] = jnp.dot(src_ref[si].astype(BF16), w_in_ref[:, f_cols],
                                   preferred_element_type=F32)
            m = jnp.min(fpre_ref[si], axis=0, keepdims=True)
            f_pre_min = m if f_pre_min is None else jnp.minimum(f_pre_min, m)
            yield
        fgt_min = jnp.min(lb + (1.0 - lb) * jax.nn.sigmoid(f_pre_min))
        flag_ref[0] = (fgt_min >= MIN_FORGET_LOCAL).astype(jnp.int32)

    @pl.when((pl.program_id(0) == 0) & (ti == 0))
    def _():
        _interleave([project_gates(x_ref)])

    local_ok = flag_ref[0] == 1

    def stream(si, fast):
        x = x_ref[si]
        xb = x.astype(BF16)
        left = jnp.dot(xb, w_in_ref[:, 0:pw + hw], preferred_element_type=F32)
        right = jnp.dot(xb, w_in_ref[:, pw + 2 * hw:pw + 4 * hw], preferred_element_type=F32)
        xp = left[:, 0:pw]
        q = left[:, pw:pw + hw]
        f_pre = fpre_ref[si]
        v = right[:, 0:hw]
        g = right[:, hw:2 * hw]
        yield

        ext = jnp.concatenate([halo_ref[si], xp], axis=0)
        halo_ref[si] = xp[tm - POOL_HALO:, :]
        ds = []
        for gi, w in enumerate(POOL_WINDOWS):
            cols = slice(gi * GROUP, (gi + 1) * GROUP)
            s = ext[:, cols]
            width = 1
            while width < w:
                s = s + pltpu.roll(s, width, 0)
                width *= 2
            ds.append((s[POOL_HALO:, :] * inv_cnt[gi] - xp[:, cols]).astype(BF16))
        fgt = lb + (1.0 - lb) * jax.nn.sigmoid(f_pre)
        lf = jnp.log(fgt)
        kk = 1.0 - fgt
        qf = q * jax.nn.sigmoid(q)
        b = _chunk_cumsum(lf)
        yield

        y_a = jnp.concatenate(
            [jnp.dot(ds[gi], pool_w_ref[gi], preferred_element_type=F32)
             for gi in range(len(POOL_WINDOWS))], axis=-1) * pool_scale_ref[...]
        yield

        if fast:
            z = _level_operands(b, fgt, qf, kk, min_half=LOCAL)
            terms = [(zl, zl, masks[li]) for li, zl in enumerate(z)]
            terms.append(_local_operands(b, qf, kk) + (local_mask,))
        else:
            z = _level_operands(b, fgt, qf, kk)
            terms = [(zl, zl, masks[li]) for li, zl in enumerate(z)]
            qk = qf * kk
        qd = (qf * jnp.exp(b)).astype(BF16)
        vb = v.astype(BF16)
        kd, dec = [], []
        for c in range(n_chunks):
            b_c = b[c * CHUNK:(c + 1) * CHUNK, :]
            b_last = b_c[CHUNK - 1:CHUNK, :]
            kd.append((kk[c * CHUNK:(c + 1) * CHUNK, :] * jnp.exp(b_last - b_c)).astype(BF16))
            dec.append(jnp.exp(b_last))
        yield

        p_lv = {}
        upd = {}
        for c in range(n_chunks):
            rows = slice(c * CHUNK, (c + 1) * CHUNK)
            for hd in range(heads):
                cols = slice(hd * GROUP, (hd + 1) * GROUP)
                p_lv[c, hd] = [lax.dot_general(lhs[rows, cols], rhs[rows, cols], NT_DIMS,
                                               preferred_element_type=F32)
                               for lhs, rhs, _ in terms]
                upd[c, hd] = lax.dot_general(vb[rows, cols], kd[c][:, cols], TN_DIMS,
                                             preferred_element_type=F32)
            yield

        o_intra = {}
        for c in range(n_chunks):
            rows = slice(c * CHUNK, (c + 1) * CHUNK)
            for hd in range(heads):
                cols = slice(hd * GROUP, (hd + 1) * GROUP)
                a = jnp.zeros((CHUNK, CHUNK), F32)
                for (_, _, mask), p in zip(terms, p_lv[c, hd]):
                    a = jnp.where(mask, p, a)
                o_intra[c, hd] = jnp.dot(a.astype(BF16), vb[rows, cols],
                                         preferred_element_type=F32)
            yield

        o_inter = {}
        for hd in range(heads):
            cols = slice(hd * GROUP, (hd + 1) * GROUP)
            st = st_ref[si, hd]
            for c in range(n_chunks):
                rows = slice(c * CHUNK, (c + 1) * CHUNK)
                o_inter[c, hd] = lax.dot_general(qd[rows, cols], st.astype(BF16), NT_DIMS,
                                                 preferred_element_type=F32)
                st = dec[c][:, cols] * st + upd[c, hd]
            st_ref[si, hd] = st
        yield

        o_heads = []
        for hd in range(heads):
            cols = slice(hd * GROUP, (hd + 1) * GROUP)
            o_h = jnp.concatenate([o_intra[c, hd] + o_inter[c, hd] for c in range(n_chunks)],
                                  axis=0)
            if not fast:
                diag = jnp.sum(qk[:, cols], axis=-1, keepdims=True)
                o_h = o_h + diag * v[:, cols]
            ms = jnp.mean(o_h * o_h, axis=-1, keepdims=True)
            g_h = g[:, cols]
            o_heads.append(o_h * lax.rsqrt(ms + RMS_EPS) * gn * (g_h * jax.nn.sigmoid(g_h)))
        y = jnp.concatenate([y_a] + o_heads, axis=-1).astype(BF16)
        yield

        mix = jnp.dot(y, w_out_ref[...], preferred_element_type=F32)
        yield
        out_ref[si] = _layer_norm(alpha * x + mix, g1_ref[...], b1_ref[...])

    def body(fast):
        _interleave([stream(si, fast) for si in range(n_s)],
                    fill=[project_gates(xn_ref)], plan=GATES_AHEAD_PLAN)

    pl.when(local_ok)(functools.partial(body, True))
    pl.when(jnp.logical_not(local_ok))(functools.partial(body, False))


def _gelu_tanh(a):
    c = 0.7978845608028654
    return 0.5 * a * (1.0 + jnp.tanh(c * (a + 0.044715 * (a * a * a))))


def _ffn_kernel(h_ref, w_up_ref, conv_w_ref, conv_b_ref, w_down_ref, g2_ref, b2_ref,
                out_ref, halo_ref, pin_ref, pout_ref, gated_ref, *, tm, alpha, d_ff):
    ti = pl.program_id(1)
    r = tm // SUBLANES
    n_cb = h_ref.shape[2] // GROUP
    n_chunks = d_ff // FF_CHUNK
    chunks = [(slice(j * FF_CHUNK, (j + 1) * FF_CHUNK),
               slice(d_ff + j * FF_CHUNK, d_ff + (j + 1) * FF_CHUNK)) for j in range(n_chunks)]

    @pl.when(ti == 0)
    def _():
        halo_ref[...] = jnp.zeros_like(halo_ref)

    def stream(si):
        for c in range(n_cb):
            pin_ref[si, c] = h_ref[si, :, c * GROUP:(c + 1) * GROUP]
        h = jnp.concatenate(
            [jnp.concatenate([pin_ref[si, c, pl.ds(s, r, stride=SUBLANES), :]
                              for c in range(n_cb)], axis=1) for s in range(SUBLANES)], axis=0)
        hb = h.astype(BF16)

        def up(cols):
            return jnp.dot(hb, w_up_ref[:, cols], preferred_element_type=F32)

        def conv(u, cols):
            shifted = []
            for k in range(CONV_WIDTH - 1):
                blk = u[(6 + k) * r:(7 + k) * r, :]
                ext = jnp.concatenate([halo_ref[si, k, :, cols], blk], axis=0)
                halo_ref[si, k, :, cols] = blk[r - CONV_HALO:, :]
                shifted.append(pltpu.roll(ext, 1, 0)[CONV_HALO:, :])
            u1 = jnp.concatenate([shifted[1], u[:7 * r, :]], axis=0)
            u2 = jnp.concatenate(shifted + [u[:6 * r, :]], axis=0)
            cw = conv_w_ref[:, cols]
            return conv_b_ref[:, cols] + cw[0:1, :] * u2 + cw[1:2, :] * u1 + cw[2:3, :] * u

        for ca, cv in chunks:
            gated_ref[si, :, ca] = (_gelu_tanh(conv(up(ca), ca)) * conv(up(cv), cv)).astype(BF16)
        yield
        acc = jnp.dot(gated_ref[si], w_down_ref[...], preferred_element_type=F32)
        res = _layer_norm(alpha * h + acc, g2_ref[...], b2_ref[...])
        for c in range(n_cb):
            for s in range(SUBLANES):
                pout_ref[si, c, pl.ds(s, r, stride=SUBLANES), :] = (
                    res[s * r:(s + 1) * r, c * GROUP:(c + 1) * GROUP])
        for c in range(n_cb):
            out_ref[si, :, c * GROUP:(c + 1) * GROUP] = pout_ref[si, c]

    _interleave([stream(si) for si in range(h_ref.shape[0])])


def _const_spec(shape):
    nd = len(shape)
    return pl.BlockSpec(shape, lambda b, t: (0,) * nd, pipeline_mode=pl.Buffered(1))


def _mixer_ln1(x, w_in, pool_w, pool_scale, lb, gnorm, w_out, g1, b1, *, tm, alpha):
    bsz, seq, d = x.shape
    heads = lb.shape[1] // GROUP
    kernel = functools.partial(_mixer_kernel, tm=tm, alpha=alpha, heads=heads)
    consts = (w_in, pool_w, pool_scale, lb, gnorm, w_out, g1, b1)
    n_t = seq // tm
    n_tiles = (bsz // STREAMS) * n_t

    def next_tile(b, t):
        g = jnp.minimum(b * n_t + t + 1, n_tiles - 1)
        return (g // n_t, g % n_t, 0)

    return pl.pallas_call(
        kernel,
        grid=(bsz // STREAMS, n_t),
        in_specs=[pl.BlockSpec((STREAMS, tm, d), lambda b, t: (b, t, 0)),
                  pl.BlockSpec((STREAMS, tm, d), next_tile)]
        + [_const_spec(c.shape) for c in consts],
        out_specs=pl.BlockSpec((STREAMS, tm, d), lambda b, t: (b, t, 0)),
        out_shape=jax.ShapeDtypeStruct(x.shape, F32),
        scratch_shapes=[
            pltpu.VMEM((STREAMS, tm, lb.shape[1]), F32),
            pltpu.SMEM((1,), jnp.int32),
            pltpu.VMEM((STREAMS, heads, GROUP, GROUP), F32),
            pltpu.VMEM((STREAMS, POOL_HALO, GROUP * len(POOL_WINDOWS)), F32),
        ],
        compiler_params=pltpu.CompilerParams(
            dimension_semantics=("arbitrary", "arbitrary"),
            vmem_limit_bytes=48 * 1024 * 1024),
        name="mixer_ln1",
    )(x, x, *consts)


def _ffn_ln2(h, w_up, conv_w, conv_b, w_down, g2, b2, *, tm, alpha):
    bsz, seq, d = h.shape
    d_ff = w_down.shape[0]
    kernel = functools.partial(_ffn_kernel, tm=tm, alpha=alpha, d_ff=d_ff)
    consts = (w_up, conv_w, conv_b, w_down, g2, b2)
    ns = FFN_STREAMS
    return pl.pallas_call(
        kernel,
        grid=(bsz // ns, seq // tm),
        in_specs=[pl.BlockSpec((ns, tm, d), lambda b, t: (b, t, 0))]
        + [_const_spec(c.shape) for c in consts],
        out_specs=pl.BlockSpec((ns, tm, d), lambda b, t: (b, t, 0)),
        out_shape=jax.ShapeDtypeStruct(h.shape, F32),
        scratch_shapes=[
            pltpu.VMEM((ns, CONV_WIDTH - 1, CONV_HALO, 2 * d_ff), F32),
            pltpu.VMEM((ns, d // GROUP, tm, GROUP), F32),
            pltpu.VMEM((ns, d // GROUP, tm, GROUP), F32),
            pltpu.VMEM((ns, tm, d_ff), BF16),
        ],
        compiler_params=pltpu.CompilerParams(
            dimension_semantics=("arbitrary", "arbitrary"),
            vmem_limit_bytes=56 * 1024 * 1024),
        name="ffn_ln2",
    )(h, *consts)


def kernel(x, w_in, pool_w, pool_scale, hgrn_lb, hgrn_gnorm, w_out, ln1_g, ln1_b,
           w_up, conv_w, conv_b, w_down, ln2_g, ln2_b):
    depth = w_in.shape[0]
    assert depth == 1 and hgrn_lb.shape[0] == 2, "single-layer block expected"
    assert w_down.shape[1] % FF_CHUNK == 0
    assert x.shape[0] % STREAMS == 0 and x.shape[0] % FFN_STREAMS == 0
    alpha = (2.0 * depth) ** 0.25
    h = _mixer_ln1(
        x, w_in[0].astype(BF16), pool_w[0].astype(BF16), pool_scale, hgrn_lb,
        hgrn_gnorm, w_out[0].astype(BF16), ln1_g, ln1_b, tm=256, alpha=alpha)
    return _ffn_ln2(
        h, w_up[0].astype(BF16), conv_w[0], conv_b, w_down[0].astype(BF16),
        ln2_g, ln2_b, tm=256, alpha=alpha)
```

```python
import functools
import math

import jax
import jax.numpy as jnp
from jax import lax
from jax.experimental import pallas as pl
from jax.experimental.pallas import tpu as pltpu

F32 = jnp.float32
BF16 = jnp.bfloat16

V7X_VMEM_BYTES = 64 * 2 ** 20
TEMPORARIES_VMEM_BYTES = 12 * 2 ** 20
STREAMS = 2
FFN_STREAMS = 2
MIXER_TILE = 256
FFN_TILE = 256
CHUNK = 64
POOL_WINDOWS = (2, 4, 8, 16)
GROUP = 128
SUBLANES = 8
POOL_HALO = 16
CONV_WIDTH = 3
CONV_HALO = 8
FF_CHUNK = 256
LN_EPS = 1e-5
RMS_EPS = 1e-6
LEVELS = (32, 16, 8, 4, 2, 1)
LOCAL = 16
MAX_LOCAL_EXPONENT = 80.0
MIN_FORGET_LOCAL = math.exp(-MAX_LOCAL_EXPONENT / LOCAL)
GATES_AHEAD_PLAN = {4: 1, 6: 1}

NT_DIMS = (((1,), (1,)), ((), ()))
TN_DIMS = (((0,), (0,)), ((), ()))


def _advance(gens):
    while gens:
        g = gens.pop(0)
        try:
            next(g)
        except StopIteration:
            continue
        gens.append(g)
        return True
    return False


def _interleave(streams, fill=(), plan=None):
    live, fill = list(streams), list(fill)
    plan = plan or {}
    n = 0
    while True:
        for _ in range(plan.get(n, 0)):
            _advance(fill)
        if not _advance(live):
            break
        n += 1
    while _advance(fill):
        pass


def _layer_norm(z, g, b):
    mu = jnp.mean(z, axis=-1, keepdims=True)
    zc = z - mu
    var = jnp.mean(zc * zc, axis=-1, keepdims=True)
    return zc * lax.rsqrt(var + LN_EPS) * g + b


def _chunk_cumsum(a):
    n, c = a.shape
    a3 = a.reshape(n // SUBLANES, SUBLANES, c)
    j = lax.broadcasted_iota(jnp.int32, (1, SUBLANES, 1), 1)
    for k in (1, 2, 4):
        a3 = a3 + jnp.where(j >= k, pltpu.roll(a3, k, 1), 0.0)
    a4 = a3.reshape(n // CHUNK, CHUNK // SUBLANES, SUBLANES, c)
    carry = None
    groups = []
    for i in range(CHUNK // SUBLANES):
        g = a4[:, i] if carry is None else a4[:, i] + carry
        carry = g[:, SUBLANES - 1:SUBLANES, :]
        groups.append(g)
    return jnp.stack(groups, axis=1).reshape(n, c)


def _local_operands(b, qf, kk):
    n, c = b.shape
    per_chunk = CHUNK // LOCAL
    b4 = b.reshape(n // CHUNK, per_chunk, LOCAL, c)
    last = b4[:, :per_chunk - 1, LOCAL - 1:LOCAL, :]
    ref = jnp.concatenate([jnp.zeros_like(last[:, :1]), last], axis=1)
    d = (b4 - ref).reshape(n, c)
    return (qf * jnp.exp(d)).astype(BF16), (kk * jnp.exp(-d)).astype(BF16)


def _level_operands(b, fgt, qf, kk, min_half=1):
    n, c = b.shape
    out = []
    for half in LEVELS:
        if half >= min_half and half >= SUBLANES:
            shape4 = (n // (2 * half), 2, half, c)
            b4, q4, k4 = b.reshape(shape4), qf.reshape(shape4), kk.reshape(shape4)
            ref = b4[:, 0, half - 1:half, :]
            z = jnp.stack([k4[:, 0] * jnp.exp(ref - b4[:, 0]),
                           q4[:, 1] * jnp.exp(b4[:, 1] - ref)], axis=1)
            out.append(z.reshape(n, c).astype(BF16))
    if min_half >= SUBLANES:
        return out
    shape3 = (n // SUBLANES, SUBLANES, c)
    j = lax.broadcasted_iota(jnp.int32, (1, SUBLANES, 1), 1)
    b3, f3, q3, k3 = (a.reshape(shape3) for a in (b, fgt, qf, kk))
    sc = jnp.exp(-jnp.abs(b3 - b3[:, 3:4, :]))
    out.append((jnp.where(j >= 4, q3, k3) * sc).reshape(n, c).astype(BF16))
    qff = q3 * f3
    prev = pltpu.roll(f3, 1, 1)
    nxt = pltpu.roll(f3, SUBLANES - 1, 1)
    j4 = j & 3
    z2 = jnp.where(j4 >= 2, qff * jnp.where(j4 == 3, prev, 1.0),
                   k3 * jnp.where(j4 == 0, nxt, 1.0))
    out.append(z2.reshape(n, c).astype(BF16))
    out.append(jnp.where((j & 1) == 1, qff, k3).reshape(n, c).astype(BF16))
    return out


def _mixer_kernel(x_ref, xn_ref, w_in_ref, pool_w_ref, pool_scale_ref, lb_ref, gnorm_ref,
                  w_out_ref, g1_ref, b1_ref, out_ref, fpre_ref, flag_ref, st_ref, halo_ref,
                  *, tm, alpha, heads):
    ti = pl.program_id(1)
    pw = GROUP * len(POOL_WINDOWS)
    hw = GROUP * heads
    n_chunks = tm // CHUNK

    @pl.when(ti == 0)
    def _():
        st_ref[...] = jnp.zeros_like(st_ref)
        halo_ref[...] = jnp.zeros_like(halo_ref)

    lbr = lb_ref[...]
    mx = jnp.max(lbr, axis=0, keepdims=True)
    e = jnp.exp(lbr - mx)
    lb = e[0:1, :] / jnp.sum(e, axis=0, keepdims=True)
    frames = (lax.broadcasted_iota(jnp.int32, (tm, 1), 0) + ti * tm + 1).astype(F32)
    inv_cnt = [1.0 / jnp.minimum(frames, float(w)) for w in POOL_WINDOWS]
    t_i = lax.broadcasted_iota(jnp.int32, (CHUNK, CHUNK), 0)
    s_i = lax.broadcasted_iota(jnp.int32, (CHUNK, CHUNK), 1)
    masks = [((t_i & half) != 0) & ((s_i & half) == 0)
             & ((t_i // (2 * half)) == (s_i // (2 * half))) for half in LEVELS]
    local_mask = ((t_i // LOCAL) == (s_i // LOCAL)) & (s_i <= t_i)
    gn = gnorm_ref[...]

    f_cols = slice(pw + hw, pw + 2 * hw)
    n_s = x_ref.shape[0]

    def project_gates(src_ref):
        f_pre_min = None
        for si in range(n_s):
            fpre_ref[si] = jnp.dot(src_ref[si].astype(BF16), w_in_ref[:, f_cols],
                                   preferred_element_type=F32)
            m = jnp.min(fpre_ref[si], axis=0, keepdims=True)
            f_pre_min = m if f_pre_min is None else jnp.minimum(f_pre_min, m)
            yield
        fgt_min = jnp.min(lb + (1.0 - lb) * jax.nn.sigmoid(f_pre_min))
        flag_ref[0] = (fgt_min >= MIN_FORGET_LOCAL).astype(jnp.int32)

    @pl.when((pl.program_id(0) == 0) & (ti == 0))
    def _():
        _interleave([project_gates(x_ref)])

    local_ok = flag_ref[0] == 1

    def stream(si, fast):
        x = x_ref[si]
        xb = x.astype(BF16)
        left = jnp.dot(xb, w_in_ref[:, 0:pw + hw], preferred_element_type=F32)
        right = jnp.dot(xb, w_in_ref[:, pw + 2 * hw:pw + 4 * hw], preferred_element_type=F32)
        xp = left[:, 0:pw]
        q = left[:, pw:pw + hw]
        f_pre = fpre_ref[si]
        v = right[:, 0:hw]
        g = right[:, hw:2 * hw]
        yield

        ext = jnp.concatenate([halo_ref[si], xp], axis=0)
        halo_ref[si] = xp[tm - POOL_HALO:, :]
        ds = []
        for gi, w in enumerate(POOL_WINDOWS):
            cols = slice(gi * GROUP, (gi + 1) * GROUP)
            s = ext[:, cols]
            width = 1
            while width < w:
                s = s + pltpu.roll(s, width, 0)
                width *= 2
            ds.append((s[POOL_HALO:, :] * inv_cnt[gi] - xp[:, cols]).astype(BF16))
        fgt = lb + (1.0 - lb) * jax.nn.sigmoid(f_pre)
        lf = jnp.log(fgt)
        kk = 1.0 - fgt
        qf = q * jax.nn.sigmoid(q)
        b = _chunk_cumsum(lf)
        yield

        y_a = jnp.concatenate(
            [jnp.dot(ds[gi], pool_w_ref[gi], preferred_element_type=F32)
             for gi in range(len(POOL_WINDOWS))], axis=-1) * pool_scale_ref[...]
        yield

        if fast:
            z = _level_operands(b, fgt, qf, kk, min_half=LOCAL)
            terms = [(zl, zl, masks[li]) for li, zl in enumerate(z)]
            terms.append(_local_operands(b, qf, kk) + (local_mask,))
        else:
            z = _level_operands(b, fgt, qf, kk)
            terms = [(zl, zl, masks[li]) for li, zl in enumerate(z)]
            qk = qf * kk
        qd = (qf * jnp.exp(b)).astype(BF16)
        vb = v.astype(BF16)
        kd, dec = [], []
        for c in range(n_chunks):
            b_c = b[c * CHUNK:(c + 1) * CHUNK, :]
            b_last = b_c[CHUNK - 1:CHUNK, :]
            kd.append((kk[c * CHUNK:(c + 1) * CHUNK, :] * jnp.exp(b_last - b_c)).astype(BF16))
            dec.append(jnp.exp(b_last))
        yield

        p_lv = {}
        upd = {}
        for c in range(n_chunks):
            rows = slice(c * CHUNK, (c + 1) * CHUNK)
            for hd in range(heads):
                cols = slice(hd * GROUP, (hd + 1) * GROUP)
                p_lv[c, hd] = [lax.dot_general(lhs[rows, cols], rhs[rows, cols], NT_DIMS,
                                               preferred_element_type=F32)
                               for lhs, rhs, _ in terms]
                upd[c, hd] = lax.dot_general(vb[rows, cols], kd[c][:, cols], TN_DIMS,
                                             preferred_element_type=F32)
            yield

        o_intra = {}
        for c in range(n_chunks):
            rows = slice(c * CHUNK, (c + 1) * CHUNK)
            for hd in range(heads):
                cols = slice(hd * GROUP, (hd + 1) * GROUP)
                a = jnp.zeros((CHUNK, CHUNK), F32)
                for (_, _, mask), p in zip(terms, p_lv[c, hd]):
                    a = jnp.where(mask, p, a)
                o_intra[c, hd] = jnp.dot(a.astype(BF16), vb[rows, cols],
                                         preferred_element_type=F32)
            yield

        o_inter = {}
        for hd in range(heads):
            cols = slice(hd * GROUP, (hd + 1) * GROUP)
            st = st_ref[si, hd]
            for c in range(n_chunks):
                rows = slice(c * CHUNK, (c + 1) * CHUNK)
                o_inter[c, hd] = lax.dot_general(qd[rows, cols], st.astype(BF16), NT_DIMS,
                                                 preferred_element_type=F32)
                st = dec[c][:, cols] * st + upd[c, hd]
            st_ref[si, hd] = st
        yield

        o_heads = []
        for hd in range(heads):
            cols = slice(hd * GROUP, (hd + 1) * GROUP)
            o_h = jnp.concatenate([o_intra[c, hd] + o_inter[c, hd] for c in range(n_chunks)],
                                  axis=0)
            if not fast:
                diag = jnp.sum(qk[:, cols], axis=-1, keepdims=True)
                o_h = o_h + diag * v[:, cols]
            ms = jnp.mean(o_h * o_h, axis=-1, keepdims=True)
            g_h = g[:, cols]
            o_heads.append(o_h * lax.rsqrt(ms + RMS_EPS) * gn * (g_h * jax.nn.sigmoid(g_h)))
        y = jnp.concatenate([y_a] + o_heads, axis=-1).astype(BF16)
        yield

        mix = jnp.dot(y, w_out_ref[...], preferred_element_type=F32)
        yield
        out_ref[si] = _layer_norm(alpha * x + mix, g1_ref[...], b1_ref[...])

    def body(fast):
        _interleave([stream(si, fast) for si in range(n_s)],
                    fill=[project_gates(xn_ref)], plan=GATES_AHEAD_PLAN)

    pl.when(local_ok)(functools.partial(body, True))
    pl.when(jnp.logical_not(local_ok))(functools.partial(body, False))


def _gelu_tanh(a):
    c = 0.7978845608028654
    return 0.5 * a * (1.0 + jnp.tanh(c * (a + 0.044715 * (a * a * a))))


def _ffn_kernel(h_ref, w_up_ref, conv_w_ref, conv_b_ref, w_down_ref, g2_ref, b2_ref,
                out_ref, halo_ref, pin_ref, pout_ref, gated_ref, *, tm, alpha, d_ff):
    ti = pl.program_id(1)
    r = tm // SUBLANES
    n_cb = h_ref.shape[2] // GROUP
    n_chunks = d_ff // FF_CHUNK
    chunks = [(slice(j * FF_CHUNK, (j + 1) * FF_CHUNK),
               slice(d_ff + j * FF_CHUNK, d_ff + (j + 1) * FF_CHUNK)) for j in range(n_chunks)]

    @pl.when(ti == 0)
    def _():
        halo_ref[...] = jnp.zeros_like(halo_ref)

    def stream(si):
        for c in range(n_cb):
            pin_ref[si, c] = h_ref[si, :, c * GROUP:(c + 1) * GROUP]
        h = jnp.concatenate(
            [jnp.concatenate([pin_ref[si, c, pl.ds(s, r, stride=SUBLANES), :]
                              for c in range(n_cb)], axis=1) for s in range(SUBLANES)], axis=0)
        hb = h.astype(BF16)

        def up(cols):
            return jnp.dot(hb, w_up_ref[:, cols], preferred_element_type=F32)

        def conv(u, cols):
            shifted = []
            for k in range(CONV_WIDTH - 1):
                blk = u[(6 + k) * r:(7 + k) * r, :]
                ext = jnp.concatenate([halo_ref[si, k, :, cols], blk], axis=0)
                halo_ref[si, k, :, cols] = blk[r - CONV_HALO:, :]
                shifted.append(pltpu.roll(ext, 1, 0)[CONV_HALO:, :])
            u1 = jnp.concatenate([shifted[1], u[:7 * r, :]], axis=0)
            u2 = jnp.concatenate(shifted + [u[:6 * r, :]], axis=0)
            cw = conv_w_ref[:, cols]
            return conv_b_ref[:, cols] + cw[0:1, :] * u2 + cw[1:2, :] * u1 + cw[2:3, :] * u

        for ca, cv in chunks:
            gated_ref[si, :, ca] = (_gelu_tanh(conv(up(ca), ca)) * conv(up(cv), cv)).astype(BF16)
        yield
        acc = jnp.dot(gated_ref[si], w_down_ref[...], preferred_element_type=F32)
        res = _layer_norm(alpha * h + acc, g2_ref[...], b2_ref[...])
        for c in range(n_cb):
            for s in range(SUBLANES):
                pout_ref[si, c, pl.ds(s, r, stride=SUBLANES), :] = (
                    res[s * r:(s + 1) * r, c * GROUP:(c + 1) * GROUP])
        for c in range(n_cb):
            out_ref[si, :, c * GROUP:(c + 1) * GROUP] = pout_ref[si, c]

    _interleave([stream(si) for si in range(h_ref.shape[0])])


def _const_spec(shape):
    nd = len(shape)
    return pl.BlockSpec(shape, lambda b, t: (0,) * nd, pipeline_mode=pl.Buffered(1))


def _nbytes(shape, dtype):
    return math.prod(shape) * jnp.dtype(dtype).itemsize


def _vmem_limit(consts, blocks, scratch):
    total = (sum(_nbytes(c.shape, c.dtype) for c in consts)
             + 2 * sum(_nbytes(*b) for b in blocks)
             + sum(_nbytes(s.shape, s.dtype) for s in scratch)
             + TEMPORARIES_VMEM_BYTES)
    assert total <= V7X_VMEM_BYTES, "tile plan does not fit VMEM"
    return total


def _mixer_ln1(x, w_in, pool_w, pool_scale, lb, gnorm, w_out, g1, b1, *, alpha):
    bsz, seq, d = x.shape
    tm = MIXER_TILE
    heads = lb.shape[1] // GROUP
    kernel = functools.partial(_mixer_kernel, tm=tm, alpha=alpha, heads=heads)
    consts = (w_in, pool_w, pool_scale, lb, gnorm, w_out, g1, b1)
    n_t = seq // tm
    n_tiles = (bsz // STREAMS) * n_t

    def next_tile(b, t):
        g = jnp.minimum(b * n_t + t + 1, n_tiles - 1)
        return (g // n_t, g % n_t, 0)

    block = (STREAMS, tm, d)
    scratch = [
        pltpu.VMEM((STREAMS, tm, lb.shape[1]), F32),
        pltpu.SMEM((1,), jnp.int32),
        pltpu.VMEM((STREAMS, heads, GROUP, GROUP), F32),
        pltpu.VMEM((STREAMS, POOL_HALO, GROUP * len(POOL_WINDOWS)), F32),
    ]
    return pl.pallas_call(
        kernel,
        grid=(bsz // STREAMS, n_t),
        in_specs=[pl.BlockSpec(block, lambda b, t: (b, t, 0)),
                  pl.BlockSpec(block, next_tile)]
        + [_const_spec(c.shape) for c in consts],
        out_specs=pl.BlockSpec(block, lambda b, t: (b, t, 0)),
        out_shape=jax.ShapeDtypeStruct(x.shape, F32),
        scratch_shapes=scratch,
        compiler_params=pltpu.CompilerParams(
            dimension_semantics=("arbitrary", "arbitrary"),
            vmem_limit_bytes=_vmem_limit(consts, [(block, F32)] * 3, scratch)),
        name="mixer_ln1",
    )(x, x, *consts)


def _ffn_ln2(h, w_up, conv_w, conv_b, w_down, g2, b2, *, alpha):
    bsz, seq, d = h.shape
    tm = FFN_TILE
    d_ff = w_down.shape[0]
    kernel = functools.partial(_ffn_kernel, tm=tm, alpha=alpha, d_ff=d_ff)
    consts = (w_up, conv_w, conv_b, w_down, g2, b2)
    ns = FFN_STREAMS
    block = (ns, tm, d)
    scratch = [
        pltpu.VMEM((ns, CONV_WIDTH - 1, CONV_HALO, 2 * d_ff), F32),
        pltpu.VMEM((ns, d // GROUP, tm, GROUP), F32),
        pltpu.VMEM((ns, d // GROUP, tm, GROUP), F32),
        pltpu.VMEM((ns, tm, d_ff), BF16),
    ]
    return pl.pallas_call(
        kernel,
        grid=(bsz // ns, seq // tm),
        in_specs=[pl.BlockSpec(block, lambda b, t: (b, t, 0))]
        + [_const_spec(c.shape) for c in consts],
        out_specs=pl.BlockSpec(block, lambda b, t: (b, t, 0)),
        out_shape=jax.ShapeDtypeStruct(h.shape, F32),
        scratch_shapes=scratch,
        compiler_params=pltpu.CompilerParams(
            dimension_semantics=("arbitrary", "arbitrary"),
            vmem_limit_bytes=_vmem_limit(consts, [(block, F32)] * 2, scratch)),
        name="ffn_ln2",
    )(h, *consts)


def kernel(x, w_in, pool_w, pool_scale, hgrn_lb, hgrn_gnorm, w_out, ln1_g, ln1_b,
           w_up, conv_w, conv_b, w_down, ln2_g, ln2_b):
    depth = w_in.shape[0]
    assert depth == 1 and hgrn_lb.shape[0] == 2, "single-layer block expected"
    assert w_down.shape[1] % FF_CHUNK == 0
    assert x.shape[0] % STREAMS == 0 and x.shape[0] % FFN_STREAMS == 0
    assert x.shape[1] % MIXER_TILE == 0 and x.shape[1] % FFN_TILE == 0
    assert MIXER_TILE % CHUNK == 0 and FFN_TILE % (SUBLANES * CONV_HALO) == 0
    alpha = (2.0 * depth) ** 0.25
    h = _mixer_ln1(
        x, w_in[0].astype(BF16), pool_w[0].astype(BF16), pool_scale, hgrn_lb,
        hgrn_gnorm, w_out[0].astype(BF16), ln1_g, ln1_b, alpha=alpha)
    return _ffn_ln2(
        h, w_up[0].astype(BF16), conv_w[0], conv_b, w_down[0].astype(BF16),
        ln2_g, ln2_b, alpha=alpha)
```

```python
import functools
import math

import jax
import jax.numpy as jnp
from jax import lax
from jax.experimental import pallas as pl
from jax.experimental.pallas import tpu as pltpu

F32 = jnp.float32
BF16 = jnp.bfloat16

V7X_VMEM_BYTES = 64 * 2 ** 20
TEMPORARIES_VMEM_BYTES = 24 * 2 ** 20
STREAMS = 2
FFN_STREAMS = 2
MIXER_TILE = 256
FFN_TILE = 256
CHUNK = 64
POOL_WINDOWS = (2, 4, 8, 16)
GROUP = 128
SUBLANES = 8
POOL_HALO = 16
CONV_WIDTH = 3
CONV_HALO = 8
FF_CHUNK = 256
LN_EPS = 1e-5
RMS_EPS = 1e-6
LEVELS = (32, 16, 8, 4, 2, 1)
LOCAL = 16
MAX_LOCAL_EXPONENT = 80.0
MIN_FORGET_LOCAL = math.exp(-MAX_LOCAL_EXPONENT / LOCAL)
GATES_AHEAD_PLAN = {4: 1, 6: 1}

NT_DIMS = (((1,), (1,)), ((), ()))
TN_DIMS = (((0,), (0,)), ((), ()))


def _advance(gens):
    while gens:
        g = gens.pop(0)
        try:
            next(g)
        except StopIteration:
            continue
        gens.append(g)
        return True
    return False


def _interleave(streams, fill=(), plan=None):
    live, fill = list(streams), list(fill)
    plan = plan or {}
    n = 0
    while True:
        for _ in range(plan.get(n, 0)):
            _advance(fill)
        if not _advance(live):
            break
        n += 1
    while _advance(fill):
        pass


def _layer_norm(z, g, b):
    mu = jnp.mean(z, axis=-1, keepdims=True)
    zc = z - mu
    var = jnp.mean(zc * zc, axis=-1, keepdims=True)
    return zc * lax.rsqrt(var + LN_EPS) * g + b


def _chunk_cumsum(a):
    n, c = a.shape
    a3 = a.reshape(n // SUBLANES, SUBLANES, c)
    j = lax.broadcasted_iota(jnp.int32, (1, SUBLANES, 1), 1)
    for k in (1, 2, 4):
        a3 = a3 + jnp.where(j >= k, pltpu.roll(a3, k, 1), 0.0)
    a4 = a3.reshape(n // CHUNK, CHUNK // SUBLANES, SUBLANES, c)
    carry = None
    groups = []
    for i in range(CHUNK // SUBLANES):
        g = a4[:, i] if carry is None else a4[:, i] + carry
        carry = g[:, SUBLANES - 1:SUBLANES, :]
        groups.append(g)
    return jnp.stack(groups, axis=1).reshape(n, c)


def _local_operands(b, qf, kk):
    n, c = b.shape
    per_chunk = CHUNK // LOCAL
    b4 = b.reshape(n // CHUNK, per_chunk, LOCAL, c)
    last = b4[:, :per_chunk - 1, LOCAL - 1:LOCAL, :]
    ref = jnp.concatenate([jnp.zeros_like(last[:, :1]), last], axis=1)
    d = (b4 - ref).reshape(n, c)
    return (qf * jnp.exp(d)).astype(BF16), (kk * jnp.exp(-d)).astype(BF16)


def _level_operands(b, fgt, qf, kk, min_half=1):
    n, c = b.shape
    out = []
    for half in LEVELS:
        if half >= min_half and half >= SUBLANES:
            shape4 = (n // (2 * half), 2, half, c)
            b4, q4, k4 = b.reshape(shape4), qf.reshape(shape4), kk.reshape(shape4)
            ref = b4[:, 0, half - 1:half, :]
            z = jnp.stack([k4[:, 0] * jnp.exp(ref - b4[:, 0]),
                           q4[:, 1] * jnp.exp(b4[:, 1] - ref)], axis=1)
            out.append(z.reshape(n, c).astype(BF16))
    if min_half >= SUBLANES:
        return out
    shape3 = (n // SUBLANES, SUBLANES, c)
    j = lax.broadcasted_iota(jnp.int32, (1, SUBLANES, 1), 1)
    b3, f3, q3, k3 = (a.reshape(shape3) for a in (b, fgt, qf, kk))
    sc = jnp.exp(-jnp.abs(b3 - b3[:, 3:4, :]))
    out.append((jnp.where(j >= 4, q3, k3) * sc).reshape(n, c).astype(BF16))
    qff = q3 * f3
    prev = pltpu.roll(f3, 1, 1)
    nxt = pltpu.roll(f3, SUBLANES - 1, 1)
    j4 = j & 3
    z2 = jnp.where(j4 >= 2, qff * jnp.where(j4 == 3, prev, 1.0),
                   k3 * jnp.where(j4 == 0, nxt, 1.0))
    out.append(z2.reshape(n, c).astype(BF16))
    out.append(jnp.where((j & 1) == 1, qff, k3).reshape(n, c).astype(BF16))
    return out


def _mixer_kernel(x_ref, xn_ref, w_in_ref, pool_w_ref, pool_scale_ref, lb_ref, gnorm_ref,
                  w_out_ref, g1_ref, b1_ref, out_ref, fpre_ref, flag_ref, st_ref, halo_ref,
                  *, tm, alpha, heads):
    ti = pl.program_id(1)
    pw = GROUP * len(POOL_WINDOWS)
    hw = GROUP * heads
    n_chunks = tm // CHUNK

    @pl.when(ti == 0)
    def _():
        st_ref[...] = jnp.zeros_like(st_ref)
        halo_ref[...] = jnp.zeros_like(halo_ref)

    lbr = lb_ref[...]
    mx = jnp.max(lbr, axis=0, keepdims=True)
    e = jnp.exp(lbr - mx)
    lb = e[0:1, :] / jnp.sum(e, axis=0, keepdims=True)
    frames = (lax.broadcasted_iota(jnp.int32, (tm, 1), 0) + ti * tm + 1).astype(F32)
    inv_cnt = [1.0 / jnp.minimum(frames, float(w)) for w in POOL_WINDOWS]
    t_i = lax.broadcasted_iota(jnp.int32, (CHUNK, CHUNK), 0)
    s_i = lax.broadcasted_iota(jnp.int32, (CHUNK, CHUNK), 1)
    masks = [((t_i & half) != 0) & ((s_i & half) == 0)
             & ((t_i // (2 * half)) == (s_i // (2 * half))) for half in LEVELS]
    local_mask = ((t_i // LOCAL) == (s_i // LOCAL)) & (s_i <= t_i)
    gn = gnorm_ref[...]

    f_cols = slice(pw + hw, pw + 2 * hw)
    n_s = x_ref.shape[0]

    def project_gates(src_ref):
        f_pre_min = None
        for si in range(n_s):
            fpre_ref[si] = jnp.dot(src_ref[si].astype(BF16), w_in_ref[:, f_cols],
                                   preferred_element_type=F32)
            m = jnp.min(fpre_ref[si], axis=0, keepdims=True)
            f_pre_min = m if f_pre_min is None else jnp.minimum(f_pre_min, m)
            yield
        fgt_min = jnp.min(lb + (1.0 - lb) * jax.nn.sigmoid(f_pre_min))
        flag_ref[0] = (fgt_min >= MIN_FORGET_LOCAL).astype(jnp.int32)

    @pl.when((pl.program_id(0) == 0) & (ti == 0))
    def _():
        _interleave([project_gates(x_ref)])

    local_ok = flag_ref[0] == 1

    def stream(si, fast):
        x = x_ref[si]
        xb = x.astype(BF16)
        left = jnp.dot(xb, w_in_ref[:, 0:pw + hw], preferred_element_type=F32)
        right = jnp.dot(xb, w_in_ref[:, pw + 2 * hw:pw + 4 * hw], preferred_element_type=F32)
        xp = left[:, 0:pw]
        q = left[:, pw:pw + hw]
        f_pre = fpre_ref[si]
        v = right[:, 0:hw]
        g = right[:, hw:2 * hw]
        yield

        ext = jnp.concatenate([halo_ref[si], xp], axis=0)
        halo_ref[si] = xp[tm - POOL_HALO:, :]
        ds = []
        for gi, w in enumerate(POOL_WINDOWS):
            cols = slice(gi * GROUP, (gi + 1) * GROUP)
            s = ext[:, cols]
            width = 1
            while width < w:
                s = s + pltpu.roll(s, width, 0)
                width *= 2
            ds.append((s[POOL_HALO:, :] * inv_cnt[gi] - xp[:, cols]).astype(BF16))
        fgt = lb + (1.0 - lb) * jax.nn.sigmoid(f_pre)
        lf = jnp.log(fgt)
        kk = 1.0 - fgt
        qf = q * jax.nn.sigmoid(q)
        b = _chunk_cumsum(lf)
        yield

        y_a = jnp.concatenate(
            [jnp.dot(ds[gi], pool_w_ref[gi], preferred_element_type=F32)
             for gi in range(len(POOL_WINDOWS))], axis=-1) * pool_scale_ref[...]
        yield

        if fast:
            z = _level_operands(b, fgt, qf, kk, min_half=LOCAL)
            terms = [(zl, zl, masks[li]) for li, zl in enumerate(z)]
            terms.append(_local_operands(b, qf, kk) + (local_mask,))
        else:
            z = _level_operands(b, fgt, qf, kk)
            terms = [(zl, zl, masks[li]) for li, zl in enumerate(z)]
            qk = qf * kk
        qd = (qf * jnp.exp(b)).astype(BF16)
        vb = v.astype(BF16)
        kd, dec = [], []
        for c in range(n_chunks):
            b_c = b[c * CHUNK:(c + 1) * CHUNK, :]
            b_last = b_c[CHUNK - 1:CHUNK, :]
            kd.append((kk[c * CHUNK:(c + 1) * CHUNK, :] * jnp.exp(b_last - b_c)).astype(BF16))
            dec.append(jnp.exp(b_last))
        yield

        p_lv = {}
        upd = {}
        for c in range(n_chunks):
            rows = slice(c * CHUNK, (c + 1) * CHUNK)
            for hd in range(heads):
                cols = slice(hd * GROUP, (hd + 1) * GROUP)
                p_lv[c, hd] = [lax.dot_general(lhs[rows, cols], rhs[rows, cols], NT_DIMS,
                                               preferred_element_type=F32)
                               for lhs, rhs, _ in terms]
                upd[c, hd] = lax.dot_general(vb[rows, cols], kd[c][:, cols], TN_DIMS,
                                             preferred_element_type=F32)
            yield

        o_intra = {}
        for c in range(n_chunks):
            rows = slice(c * CHUNK, (c + 1) * CHUNK)
            for hd in range(heads):
                cols = slice(hd * GROUP, (hd + 1) * GROUP)
                a = jnp.zeros((CHUNK, CHUNK), F32)
                for (_, _, mask), p in zip(terms, p_lv[c, hd]):
                    a = jnp.where(mask, p, a)
                o_intra[c, hd] = jnp.dot(a.astype(BF16), vb[rows, cols],
                                         preferred_element_type=F32)
            yield

        o_inter = {}
        for hd in range(heads):
            cols = slice(hd * GROUP, (hd + 1) * GROUP)
            st = st_ref[si, hd]
            for c in range(n_chunks):
                rows = slice(c * CHUNK, (c + 1) * CHUNK)
                o_inter[c, hd] = lax.dot_general(qd[rows, cols], st.astype(BF16), NT_DIMS,
                                                 preferred_element_type=F32)
                st = dec[c][:, cols] * st + upd[c, hd]
            st_ref[si, hd] = st
        yield

        o_heads = []
        for hd in range(heads):
            cols = slice(hd * GROUP, (hd + 1) * GROUP)
            o_h = jnp.concatenate([o_intra[c, hd] + o_inter[c, hd] for c in range(n_chunks)],
                                  axis=0)
            if not fast:
                diag = jnp.sum(qk[:, cols], axis=-1, keepdims=True)
                o_h = o_h + diag * v[:, cols]
            ms = jnp.mean(o_h * o_h, axis=-1, keepdims=True)
            g_h = g[:, cols]
            o_heads.append(o_h * lax.rsqrt(ms + RMS_EPS) * gn * (g_h * jax.nn.sigmoid(g_h)))
        y = jnp.concatenate([y_a] + o_heads, axis=-1).astype(BF16)
        yield

        mix = jnp.dot(y, w_out_ref[...], preferred_element_type=F32)
        yield
        out_ref[si] = _layer_norm(alpha * x + mix, g1_ref[...], b1_ref[...])

    def body(fast):
        _interleave([stream(si, fast) for si in range(n_s)],
                    fill=[project_gates(xn_ref)], plan=GATES_AHEAD_PLAN)

    pl.when(local_ok)(functools.partial(body, True))
    pl.when(jnp.logical_not(local_ok))(functools.partial(body, False))


def _gelu_tanh(a):
    c = 0.7978845608028654
    return 0.5 * a * (1.0 + jnp.tanh(c * (a + 0.044715 * (a * a * a))))


def _ffn_kernel(h_ref, w_up_ref, conv_w_ref, conv_b_ref, w_down_ref, g2_ref, b2_ref,
                out_ref, halo_ref, pin_ref, pout_ref, gated_ref, *, tm, alpha, d_ff):
    ti = pl.program_id(1)
    r = tm // SUBLANES
    n_cb = h_ref.shape[2] // GROUP
    n_chunks = d_ff // FF_CHUNK
    chunks = [(slice(j * FF_CHUNK, (j + 1) * FF_CHUNK),
               slice(d_ff + j * FF_CHUNK, d_ff + (j + 1) * FF_CHUNK)) for j in range(n_chunks)]

    @pl.when(ti == 0)
    def _():
        halo_ref[...] = jnp.zeros_like(halo_ref)

    def stream(si):
        for c in range(n_cb):
            pin_ref[si, c] = h_ref[si, :, c * GROUP:(c + 1) * GROUP]
        h = jnp.concatenate(
            [jnp.concatenate([pin_ref[si, c, pl.ds(s, r, stride=SUBLANES), :]
                              for c in range(n_cb)], axis=1) for s in range(SUBLANES)], axis=0)
        hb = h.astype(BF16)

        def up(cols):
            return jnp.dot(hb, w_up_ref[:, cols], preferred_element_type=F32)

        def conv(u, cols):
            shifted = []
            for k in range(CONV_WIDTH - 1):
                blk = u[(6 + k) * r:(7 + k) * r, :]
                ext = jnp.concatenate([halo_ref[si, k, :, cols], blk], axis=0)
                halo_ref[si, k, :, cols] = blk[r - CONV_HALO:, :]
                shifted.append(pltpu.roll(ext, 1, 0)[CONV_HALO:, :])
            u1 = jnp.concatenate([shifted[1], u[:7 * r, :]], axis=0)
            u2 = jnp.concatenate(shifted + [u[:6 * r, :]], axis=0)
            cw = conv_w_ref[:, cols]
            return conv_b_ref[:, cols] + cw[0:1, :] * u2 + cw[1:2, :] * u1 + cw[2:3, :] * u

        for ca, cv in chunks:
            gated_ref[si, :, ca] = (_gelu_tanh(conv(up(ca), ca)) * conv(up(cv), cv)).astype(BF16)
        yield
        acc = jnp.dot(gated_ref[si], w_down_ref[...], preferred_element_type=F32)
        res = _layer_norm(alpha * h + acc, g2_ref[...], b2_ref[...])
        for c in range(n_cb):
            for s in range(SUBLANES):
                pout_ref[si, c, pl.ds(s, r, stride=SUBLANES), :] = (
                    res[s * r:(s + 1) * r, c * GROUP:(c + 1) * GROUP])
        for c in range(n_cb):
            out_ref[si, :, c * GROUP:(c + 1) * GROUP] = pout_ref[si, c]

    _interleave([stream(si) for si in range(h_ref.shape[0])])


def _const_spec(shape):
    nd = len(shape)
    return pl.BlockSpec(shape, lambda b, t: (0,) * nd, pipeline_mode=pl.Buffered(1))


def _nbytes(shape, dtype):
    return math.prod(shape) * jnp.dtype(dtype).itemsize


def _vmem_limit(consts, blocks, scratch):
    total = (sum(_nbytes(c.shape, c.dtype) for c in consts)
             + 2 * sum(_nbytes(*b) for b in blocks)
             + sum(_nbytes(s.shape, s.dtype) for s in scratch)
             + TEMPORARIES_VMEM_BYTES)
    assert total <= V7X_VMEM_BYTES, "tile plan does not fit VMEM"
    return total


def _mixer_ln1(x, w_in, pool_w, pool_scale, lb, gnorm, w_out, g1, b1, *, alpha):
    bsz, seq, d = x.shape
    tm = MIXER_TILE
    heads = lb.shape[1] // GROUP
    kernel = functools.partial(_mixer_kernel, tm=tm, alpha=alpha, heads=heads)
    consts = (w_in, pool_w, pool_scale, lb, gnorm, w_out, g1, b1)
    n_t = seq // tm
    n_tiles = (bsz // STREAMS) * n_t

    def next_tile(b, t):
        g = jnp.minimum(b * n_t + t + 1, n_tiles - 1)
        return (g // n_t, g % n_t, 0)

    block = (STREAMS, tm, d)
    scratch = [
        pltpu.VMEM((STREAMS, tm, lb.shape[1]), F32),
        pltpu.SMEM((1,), jnp.int32),
        pltpu.VMEM((STREAMS, heads, GROUP, GROUP), F32),
        pltpu.VMEM((STREAMS, POOL_HALO, GROUP * len(POOL_WINDOWS)), F32),
    ]
    return pl.pallas_call(
        kernel,
        grid=(bsz // STREAMS, n_t),
        in_specs=[pl.BlockSpec(block, lambda b, t: (b, t, 0)),
                  pl.BlockSpec(block, next_tile)]
        + [_const_spec(c.shape) for c in consts],
        out_specs=pl.BlockSpec(block, lambda b, t: (b, t, 0)),
        out_shape=jax.ShapeDtypeStruct(x.shape, F32),
        scratch_shapes=scratch,
        compiler_params=pltpu.CompilerParams(
            dimension_semantics=("arbitrary", "arbitrary"),
            vmem_limit_bytes=_vmem_limit(consts, [(block, F32)] * 3, scratch)),
        name="mixer_ln1",
    )(x, x, *consts)


def _ffn_ln2(h, w_up, conv_w, conv_b, w_down, g2, b2, *, alpha):
    bsz, seq, d = h.shape
    tm = FFN_TILE
    d_ff = w_down.shape[0]
    kernel = functools.partial(_ffn_kernel, tm=tm, alpha=alpha, d_ff=d_ff)
    consts = (w_up, conv_w, conv_b, w_down, g2, b2)
    ns = FFN_STREAMS
    block = (ns, tm, d)
    scratch = [
        pltpu.VMEM((ns, CONV_WIDTH - 1, CONV_HALO, 2 * d_ff), F32),
        pltpu.VMEM((ns, d // GROUP, tm, GROUP), F32),
        pltpu.VMEM((ns, d // GROUP, tm, GROUP), F32),
        pltpu.VMEM((ns, tm, d_ff), BF16),
    ]
    return pl.pallas_call(
        kernel,
        grid=(bsz // ns, seq // tm),
        in_specs=[pl.BlockSpec(block, lambda b, t: (b, t, 0))]
        + [_const_spec(c.shape) for c in consts],
        out_specs=pl.BlockSpec(block, lambda b, t: (b, t, 0)),
        out_shape=jax.ShapeDtypeStruct(h.shape, F32),
        scratch_shapes=scratch,
        compiler_params=pltpu.CompilerParams(
            dimension_semantics=("arbitrary", "arbitrary"),
            vmem_limit_bytes=_vmem_limit(consts, [(block, F32)] * 2, scratch)),
        name="ffn_ln2",
    )(h, *consts)


def kernel(x, w_in, pool_w, pool_scale, hgrn_lb, hgrn_gnorm, w_out, ln1_g, ln1_b,
           w_up, conv_w, conv_b, w_down, ln2_g, ln2_b):
    depth = w_in.shape[0]
    assert depth == 1 and hgrn_lb.shape[0] == 2, "single-layer block expected"
    assert w_down.shape[1] % FF_CHUNK == 0
    assert x.shape[0] % STREAMS == 0 and x.shape[0] % FFN_STREAMS == 0
    assert x.shape[1] % MIXER_TILE == 0 and x.shape[1] % FFN_TILE == 0
    assert MIXER_TILE % CHUNK == 0 and FFN_TILE % (SUBLANES * CONV_HALO) == 0
    alpha = (2.0 * depth) ** 0.25
    h = _mixer_ln1(
        x, w_in[0].astype(BF16), pool_w[0].astype(BF16), pool_scale, hgrn_lb,
        hgrn_gnorm, w_out[0].astype(BF16), ln1_g, ln1_b, alpha=alpha)
    return _ffn_ln2(
        h, w_up[0].astype(BF16), conv_w[0], conv_b, w_down[0].astype(BF16),
        ln2_g, ln2_b, alpha=alpha)
```

```python
import functools
import math

import jax
import jax.numpy as jnp
from jax import lax
from jax.experimental import pallas as pl
from jax.experimental.pallas import tpu as pltpu

F32 = jnp.float32
BF16 = jnp.bfloat16

V7X_VMEM_BYTES = 64 * 2 ** 20
TEMPORARIES_VMEM_BYTES = 28 * 2 ** 20
STREAMS = 2
FFN_STREAMS = 2
MIXER_TILE = 256
FFN_TILE = 256
CHUNK = 64
POOL_WINDOWS = (2, 4, 8, 16)
GROUP = 128
SUBLANES = 8
POOL_HALO = 16
CONV_WIDTH = 3
CONV_HALO = 8
FF_CHUNK = 256
LN_EPS = 1e-5
RMS_EPS = 1e-6
LEVELS = (32, 16, 8, 4, 2, 1)
LOCAL = 16
MAX_LOCAL_EXPONENT = 80.0
MIN_FORGET_LOCAL = math.exp(-MAX_LOCAL_EXPONENT / LOCAL)
GATES_AHEAD_PLAN = {4: 1, 6: 1}

NT_DIMS = (((1,), (1,)), ((), ()))
TN_DIMS = (((0,), (0,)), ((), ()))


def _advance(gens):
    while gens:
        g = gens.pop(0)
        try:
            next(g)
        except StopIteration:
            continue
        gens.append(g)
        return True
    return False


def _interleave(streams, fill=(), plan=None):
    live, fill = list(streams), list(fill)
    plan = plan or {}
    n = 0
    while True:
        for _ in range(plan.get(n, 0)):
            _advance(fill)
        if not _advance(live):
            break
        n += 1
    while _advance(fill):
        pass


def _layer_norm(z, g, b):
    mu = jnp.mean(z, axis=-1, keepdims=True)
    zc = z - mu
    var = jnp.mean(zc * zc, axis=-1, keepdims=True)
    return zc * lax.rsqrt(var + LN_EPS) * g + b


def _chunk_cumsum(a):
    n, c = a.shape
    a3 = a.reshape(n // SUBLANES, SUBLANES, c)
    j = lax.broadcasted_iota(jnp.int32, (1, SUBLANES, 1), 1)
    for k in (1, 2, 4):
        a3 = a3 + jnp.where(j >= k, pltpu.roll(a3, k, 1), 0.0)
    a4 = a3.reshape(n // CHUNK, CHUNK // SUBLANES, SUBLANES, c)
    carry = None
    groups = []
    for i in range(CHUNK // SUBLANES):
        g = a4[:, i] if carry is None else a4[:, i] + carry
        carry = g[:, SUBLANES - 1:SUBLANES, :]
        groups.append(g)
    return jnp.stack(groups, axis=1).reshape(n, c)


def _local_operands(b, qf, kk):
    n, c = b.shape
    per_chunk = CHUNK // LOCAL
    b4 = b.reshape(n // CHUNK, per_chunk, LOCAL, c)
    last = b4[:, :per_chunk - 1, LOCAL - 1:LOCAL, :]
    ref = jnp.concatenate([jnp.zeros_like(last[:, :1]), last], axis=1)
    d = (b4 - ref).reshape(n, c)
    return (qf * jnp.exp(d)).astype(BF16), (kk * jnp.exp(-d)).astype(BF16)


def _level_operands(b, fgt, qf, kk, min_half=1):
    n, c = b.shape
    out = []
    for half in LEVELS:
        if half >= min_half and half >= SUBLANES:
            shape4 = (n // (2 * half), 2, half, c)
            b4, q4, k4 = b.reshape(shape4), qf.reshape(shape4), kk.reshape(shape4)
            ref = b4[:, 0, half - 1:half, :]
            z = jnp.stack([k4[:, 0] * jnp.exp(ref - b4[:, 0]),
                           q4[:, 1] * jnp.exp(b4[:, 1] - ref)], axis=1)
            out.append(z.reshape(n, c).astype(BF16))
    if min_half >= SUBLANES:
        return out
    shape3 = (n // SUBLANES, SUBLANES, c)
    j = lax.broadcasted_iota(jnp.int32, (1, SUBLANES, 1), 1)
    b3, f3, q3, k3 = (a.reshape(shape3) for a in (b, fgt, qf, kk))
    sc = jnp.exp(-jnp.abs(b3 - b3[:, 3:4, :]))
    out.append((jnp.where(j >= 4, q3, k3) * sc).reshape(n, c).astype(BF16))
    qff = q3 * f3
    prev = pltpu.roll(f3, 1, 1)
    nxt = pltpu.roll(f3, SUBLANES - 1, 1)
    j4 = j & 3
    z2 = jnp.where(j4 >= 2, qff * jnp.where(j4 == 3, prev, 1.0),
                   k3 * jnp.where(j4 == 0, nxt, 1.0))
    out.append(z2.reshape(n, c).astype(BF16))
    out.append(jnp.where((j & 1) == 1, qff, k3).reshape(n, c).astype(BF16))
    return out


def _mixer_kernel(x_ref, xn_ref, w_in_ref, pool_w_ref, pool_scale_ref, lb_ref, gnorm_ref,
                  w_out_ref, g1_ref, b1_ref, out_ref, fpre_ref, flag_ref, st_ref, halo_ref,
                  *, tm, alpha, heads):
    ti = pl.program_id(1)
    pw = GROUP * len(POOL_WINDOWS)
    hw = GROUP * heads
    n_chunks = tm // CHUNK

    @pl.when(ti == 0)
    def _():
        st_ref[...] = jnp.zeros_like(st_ref)
        halo_ref[...] = jnp.zeros_like(halo_ref)

    lbr = lb_ref[...]
    mx = jnp.max(lbr, axis=0, keepdims=True)
    e = jnp.exp(lbr - mx)
    lb = e[0:1, :] / jnp.sum(e, axis=0, keepdims=True)
    frames = (lax.broadcasted_iota(jnp.int32, (tm, 1), 0) + ti * tm + 1).astype(F32)
    inv_cnt = [1.0 / jnp.minimum(frames, float(w)) for w in POOL_WINDOWS]
    t_i = lax.broadcasted_iota(jnp.int32, (CHUNK, CHUNK), 0)
    s_i = lax.broadcasted_iota(jnp.int32, (CHUNK, CHUNK), 1)
    masks = [((t_i & half) != 0) & ((s_i & half) == 0)
             & ((t_i // (2 * half)) == (s_i // (2 * half))) for half in LEVELS]
    local_mask = ((t_i // LOCAL) == (s_i // LOCAL)) & (s_i <= t_i)
    gn = gnorm_ref[...]

    f_cols = slice(pw + hw, pw + 2 * hw)
    n_s = x_ref.shape[0]

    def project_gates(src_ref):
        f_pre_min = None
        for si in range(n_s):
            fpre_ref[si] = jnp.dot(src_ref[si].astype(BF16), w_in_ref[:, f_cols],
                                   preferred_element_type=F32)
            m = jnp.min(fpre_ref[si], axis=0, keepdims=True)
            f_pre_min = m if f_pre_min is None else jnp.minimum(f_pre_min, m)
            yield
        fgt_min = jnp.min(lb + (1.0 - lb) * jax.nn.sigmoid(f_pre_min))
        flag_ref[0] = (fgt_min >= MIN_FORGET_LOCAL).astype(jnp.int32)

    @pl.when((pl.program_id(0) == 0) & (ti == 0))
    def _():
        _interleave([project_gates(x_ref)])

    local_ok = flag_ref[0] == 1

    def stream(si, fast):
        x = x_ref[si]
        xb = x.astype(BF16)
        left = jnp.dot(xb, w_in_ref[:, 0:pw + hw], preferred_element_type=F32)
        right = jnp.dot(xb, w_in_ref[:, pw + 2 * hw:pw + 4 * hw], preferred_element_type=F32)
        xp = left[:, 0:pw]
        q = left[:, pw:pw + hw]
        f_pre = fpre_ref[si]
        v = right[:, 0:hw]
        g = right[:, hw:2 * hw]
        yield

        ext = jnp.concatenate([halo_ref[si], xp], axis=0)
        halo_ref[si] = xp[tm - POOL_HALO:, :]
        ds = []
        for gi, w in enumerate(POOL_WINDOWS):
            cols = slice(gi * GROUP, (gi + 1) * GROUP)
            s = ext[:, cols]
            width = 1
            while width < w:
                s = s + pltpu.roll(s, width, 0)
                width *= 2
            ds.append((s[POOL_HALO:, :] * inv_cnt[gi] - xp[:, cols]).astype(BF16))
        fgt = lb + (1.0 - lb) * jax.nn.sigmoid(f_pre)
        lf = jnp.log(fgt)
        kk = 1.0 - fgt
        qf = q * jax.nn.sigmoid(q)
        b = _chunk_cumsum(lf)
        yield

        y_a = jnp.concatenate(
            [jnp.dot(ds[gi], pool_w_ref[gi], preferred_element_type=F32)
             for gi in range(len(POOL_WINDOWS))], axis=-1) * pool_scale_ref[...]
        yield

        if fast:
            z = _level_operands(b, fgt, qf, kk, min_half=LOCAL)
            terms = [(zl, zl, masks[li]) for li, zl in enumerate(z)]
            terms.append(_local_operands(b, qf, kk) + (local_mask,))
        else:
            z = _level_operands(b, fgt, qf, kk)
            terms = [(zl, zl, masks[li]) for li, zl in enumerate(z)]
            qk = qf * kk
        qd = (qf * jnp.exp(b)).astype(BF16)
        vb = v.astype(BF16)
        kd, dec = [], []
        for c in range(n_chunks):
            b_c = b[c * CHUNK:(c + 1) * CHUNK, :]
            b_last = b_c[CHUNK - 1:CHUNK, :]
            kd.append((kk[c * CHUNK:(c + 1) * CHUNK, :] * jnp.exp(b_last - b_c)).astype(BF16))
            dec.append(jnp.exp(b_last))
        yield

        p_lv = {}
        upd = {}
        for c in range(n_chunks):
            rows = slice(c * CHUNK, (c + 1) * CHUNK)
            for hd in range(heads):
                cols = slice(hd * GROUP, (hd + 1) * GROUP)
                p_lv[c, hd] = [lax.dot_general(lhs[rows, cols], rhs[rows, cols], NT_DIMS,
                                               preferred_element_type=F32)
                               for lhs, rhs, _ in terms]
                upd[c, hd] = lax.dot_general(vb[rows, cols], kd[c][:, cols], TN_DIMS,
                                             preferred_element_type=F32)
            yield

        o_intra = {}
        for c in range(n_chunks):
            rows = slice(c * CHUNK, (c + 1) * CHUNK)
            for hd in range(heads):
                cols = slice(hd * GROUP, (hd + 1) * GROUP)
                a = jnp.zeros((CHUNK, CHUNK), F32)
                for (_, _, mask), p in zip(terms, p_lv[c, hd]):
                    a = jnp.where(mask, p, a)
                o_intra[c, hd] = jnp.dot(a.astype(BF16), vb[rows, cols],
                                         preferred_element_type=F32)
            yield

        o_inter = {}
        for hd in range(heads):
            cols = slice(hd * GROUP, (hd + 1) * GROUP)
            st = st_ref[si, hd]
            for c in range(n_chunks):
                rows = slice(c * CHUNK, (c + 1) * CHUNK)
                o_inter[c, hd] = lax.dot_general(qd[rows, cols], st.astype(BF16), NT_DIMS,
                                                 preferred_element_type=F32)
                st = dec[c][:, cols] * st + upd[c, hd]
            st_ref[si, hd] = st
        yield

        o_heads = []
        for hd in range(heads):
            cols = slice(hd * GROUP, (hd + 1) * GROUP)
            o_h = jnp.concatenate([o_intra[c, hd] + o_inter[c, hd] for c in range(n_chunks)],
                                  axis=0)
            if not fast:
                diag = jnp.sum(qk[:, cols], axis=-1, keepdims=True)
                o_h = o_h + diag * v[:, cols]
            ms = jnp.mean(o_h * o_h, axis=-1, keepdims=True)
            g_h = g[:, cols]
            o_heads.append(o_h * lax.rsqrt(ms + RMS_EPS) * gn * (g_h * jax.nn.sigmoid(g_h)))
        y = jnp.concatenate([y_a] + o_heads, axis=-1).astype(BF16)
        yield

        mix = jnp.dot(y, w_out_ref[...], preferred_element_type=F32)
        yield
        out_ref[si] = _layer_norm(alpha * x + mix, g1_ref[...], b1_ref[...])

    def body(fast):
        _interleave([stream(si, fast) for si in range(n_s)],
                    fill=[project_gates(xn_ref)], plan=GATES_AHEAD_PLAN)

    pl.when(local_ok)(functools.partial(body, True))
    pl.when(jnp.logical_not(local_ok))(functools.partial(body, False))


def _gelu_tanh(a):
    c = 0.7978845608028654
    return 0.5 * a * (1.0 + jnp.tanh(c * (a + 0.044715 * (a * a * a))))


def _ffn_kernel(h_ref, w_up_ref, conv_w_ref, conv_b_ref, w_down_ref, g2_ref, b2_ref,
                out_ref, halo_ref, pin_ref, pout_ref, gated_ref, *, tm, alpha, d_ff):
    ti = pl.program_id(1)
    r = tm // SUBLANES
    n_cb = h_ref.shape[2] // GROUP
    n_chunks = d_ff // FF_CHUNK
    chunks = [(slice(j * FF_CHUNK, (j + 1) * FF_CHUNK),
               slice(d_ff + j * FF_CHUNK, d_ff + (j + 1) * FF_CHUNK)) for j in range(n_chunks)]

    @pl.when(ti == 0)
    def _():
        halo_ref[...] = jnp.zeros_like(halo_ref)

    def stream(si):
        for c in range(n_cb):
            pin_ref[si, c] = h_ref[si, :, c * GROUP:(c + 1) * GROUP]
        h = jnp.concatenate(
            [jnp.concatenate([pin_ref[si, c, pl.ds(s, r, stride=SUBLANES), :]
                              for c in range(n_cb)], axis=1) for s in range(SUBLANES)], axis=0)
        hb = h.astype(BF16)

        def up(cols):
            return jnp.dot(hb, w_up_ref[:, cols], preferred_element_type=F32)

        def conv(u, cols):
            shifted = []
            for k in range(CONV_WIDTH - 1):
                blk = u[(6 + k) * r:(7 + k) * r, :]
                ext = jnp.concatenate([halo_ref[si, k, :, cols], blk], axis=0)
                halo_ref[si, k, :, cols] = blk[r - CONV_HALO:, :]
                shifted.append(pltpu.roll(ext, 1, 0)[CONV_HALO:, :])
            u1 = jnp.concatenate([shifted[1], u[:7 * r, :]], axis=0)
            u2 = jnp.concatenate(shifted + [u[:6 * r, :]], axis=0)
            cw = conv_w_ref[:, cols]
            return conv_b_ref[:, cols] + cw[0:1, :] * u2 + cw[1:2, :] * u1 + cw[2:3, :] * u

        for ca, cv in chunks:
            gated_ref[si, :, ca] = (_gelu_tanh(conv(up(ca), ca)) * conv(up(cv), cv)).astype(BF16)
        yield
        acc = jnp.dot(gated_ref[si], w_down_ref[...], preferred_element_type=F32)
        res = _layer_norm(alpha * h + acc, g2_ref[...], b2_ref[...])
        for c in range(n_cb):
            for s in range(SUBLANES):
                pout_ref[si, c, pl.ds(s, r, stride=SUBLANES), :] = (
                    res[s * r:(s + 1) * r, c * GROUP:(c + 1) * GROUP])
        for c in range(n_cb):
            out_ref[si, :, c * GROUP:(c + 1) * GROUP] = pout_ref[si, c]

    _interleave([stream(si) for si in range(h_ref.shape[0])])


def _const_spec(shape):
    nd = len(shape)
    return pl.BlockSpec(shape, lambda b, t: (0,) * nd, pipeline_mode=pl.Buffered(1))


def _nbytes(shape, dtype):
    return math.prod(shape) * jnp.dtype(dtype).itemsize


def _vmem_limit(consts, blocks, scratch):
    total = (sum(_nbytes(c.shape, c.dtype) for c in consts)
             + 2 * sum(_nbytes(*b) for b in blocks)
             + sum(_nbytes(s.shape, s.dtype) for s in scratch)
             + TEMPORARIES_VMEM_BYTES)
    assert total <= V7X_VMEM_BYTES, "tile plan does not fit VMEM"
    return total


def _mixer_ln1(x, w_in, pool_w, pool_scale, lb, gnorm, w_out, g1, b1, *, alpha):
    bsz, seq, d = x.shape
    tm = MIXER_TILE
    heads = lb.shape[1] // GROUP
    kernel = functools.partial(_mixer_kernel, tm=tm, alpha=alpha, heads=heads)
    consts = (w_in, pool_w, pool_scale, lb, gnorm, w_out, g1, b1)
    n_t = seq // tm
    n_tiles = (bsz // STREAMS) * n_t

    def next_tile(b, t):
        g = jnp.minimum(b * n_t + t + 1, n_tiles - 1)
        return (g // n_t, g % n_t, 0)

    block = (STREAMS, tm, d)
    scratch = [
        pltpu.VMEM((STREAMS, tm, lb.shape[1]), F32),
        pltpu.SMEM((1,), jnp.int32),
        pltpu.VMEM((STREAMS, heads, GROUP, GROUP), F32),
        pltpu.VMEM((STREAMS, POOL_HALO, GROUP * len(POOL_WINDOWS)), F32),
    ]
    return pl.pallas_call(
        kernel,
        grid=(bsz // STREAMS, n_t),
        in_specs=[pl.BlockSpec(block, lambda b, t: (b, t, 0)),
                  pl.BlockSpec(block, next_tile)]
        + [_const_spec(c.shape) for c in consts],
        out_specs=pl.BlockSpec(block, lambda b, t: (b, t, 0)),
        out_shape=jax.ShapeDtypeStruct(x.shape, F32),
        scratch_shapes=scratch,
        compiler_params=pltpu.CompilerParams(
            dimension_semantics=("arbitrary", "arbitrary"),
            vmem_limit_bytes=_vmem_limit(consts, [(block, F32)] * 3, scratch)),
        name="mixer_ln1",
    )(x, x, *consts)


def _ffn_ln2(h, w_up, conv_w, conv_b, w_down, g2, b2, *, alpha):
    bsz, seq, d = h.shape
    tm = FFN_TILE
    d_ff = w_down.shape[0]
    kernel = functools.partial(_ffn_kernel, tm=tm, alpha=alpha, d_ff=d_ff)
    consts = (w_up, conv_w, conv_b, w_down, g2, b2)
    ns = FFN_STREAMS
    block = (ns, tm, d)
    scratch = [
        pltpu.VMEM((ns, CONV_WIDTH - 1, CONV_HALO, 2 * d_ff), F32),
        pltpu.VMEM((ns, d // GROUP, tm, GROUP), F32),
        pltpu.VMEM((ns, d // GROUP, tm, GROUP), F32),
        pltpu.VMEM((ns, tm, d_ff), BF16),
    ]
    return pl.pallas_call(
        kernel,
        grid=(bsz // ns, seq // tm),
        in_specs=[pl.BlockSpec(block, lambda b, t: (b, t, 0))]
        + [_const_spec(c.shape) for c in consts],
        out_specs=pl.BlockSpec(block, lambda b, t: (b, t, 0)),
        out_shape=jax.ShapeDtypeStruct(h.shape, F32),
        scratch_shapes=scratch,
        compiler_params=pltpu.CompilerParams(
            dimension_semantics=("arbitrary", "arbitrary"),
            vmem_limit_bytes=_vmem_limit(consts, [(block, F32)] * 2, scratch)),
        name="ffn_ln2",
    )(h, *consts)


def kernel(x, w_in, pool_w, pool_scale, hgrn_lb, hgrn_gnorm, w_out, ln1_g, ln1_b,
           w_up, conv_w, conv_b, w_down, ln2_g, ln2_b):
    depth = w_in.shape[0]
    assert depth == 1 and hgrn_lb.shape[0] == 2, "single-layer block expected"
    assert w_down.shape[1] % FF_CHUNK == 0
    assert x.shape[0] % STREAMS == 0 and x.shape[0] % FFN_STREAMS == 0
    assert x.shape[1] % MIXER_TILE == 0 and x.shape[1] % FFN_TILE == 0
    assert MIXER_TILE % CHUNK == 0 and FFN_TILE % (SUBLANES * CONV_HALO) == 0
    alpha = (2.0 * depth) ** 0.25
    h = _mixer_ln1(
        x, w_in[0].astype(BF16), pool_w[0].astype(BF16), pool_scale, hgrn_lb,
        hgrn_gnorm, w_out[0].astype(BF16), ln1_g, ln1_b, alpha=alpha)
    return _ffn_ln2(
        h, w_up[0].astype(BF16), conv_w[0], conv_b, w_down[0].astype(BF16),
        ln2_g, ln2_b, alpha=alpha)
```

```python
import functools
import math

import jax
import jax.numpy as jnp
from jax import lax
from jax.experimental import pallas as pl
from jax.experimental.pallas import tpu as pltpu

F32 = jnp.float32
BF16 = jnp.bfloat16

V7X_VMEM_BYTES = 64 * 2 ** 20
TEMPORARIES_VMEM_BYTES = 24 * 2 ** 20
STREAMS = 2
FFN_STREAMS = 2
MIXER_TILE = 256
FFN_TILE = 256
CHUNK = 64
POOL_WINDOWS = (2, 4, 8, 16)
GROUP = 128
SUBLANES = 8
POOL_HALO = 16
CONV_WIDTH = 3
CONV_HALO = 8
FF_CHUNK = 256
LN_EPS = 1e-5
RMS_EPS = 1e-6
LEVELS = (32, 16, 8, 4, 2, 1)
LOCAL = 16
MAX_LOCAL_EXPONENT = 80.0
MIN_FORGET_LOCAL = math.exp(-MAX_LOCAL_EXPONENT / LOCAL)
GATES_AHEAD_PLAN = {4: 1, 6: 1}

NT_DIMS = (((1,), (1,)), ((), ()))
TN_DIMS = (((0,), (0,)), ((), ()))


def _advance(gens):
    while gens:
        g = gens.pop(0)
        try:
            next(g)
        except StopIteration:
            continue
        gens.append(g)
        return True
    return False


def _interleave(streams, fill=(), plan=None):
    live, fill = list(streams), list(fill)
    plan = plan or {}
    n = 0
    while True:
        for _ in range(plan.get(n, 0)):
            _advance(fill)
        if not _advance(live):
            break
        n += 1
    while _advance(fill):
        pass


def _layer_norm(z, g, b):
    mu = jnp.mean(z, axis=-1, keepdims=True)
    zc = z - mu
    var = jnp.mean(zc * zc, axis=-1, keepdims=True)
    return zc * lax.rsqrt(var + LN_EPS) * g + b


def _chunk_cumsum(a):
    n, c = a.shape
    a3 = a.reshape(n // SUBLANES, SUBLANES, c)
    j = lax.broadcasted_iota(jnp.int32, (1, SUBLANES, 1), 1)
    for k in (1, 2, 4):
        a3 = a3 + jnp.where(j >= k, pltpu.roll(a3, k, 1), 0.0)
    a4 = a3.reshape(n // CHUNK, CHUNK // SUBLANES, SUBLANES, c)
    carry = None
    groups = []
    for i in range(CHUNK // SUBLANES):
        g = a4[:, i] if carry is None else a4[:, i] + carry
        carry = g[:, SUBLANES - 1:SUBLANES, :]
        groups.append(g)
    return jnp.stack(groups, axis=1).reshape(n, c)


def _local_operands(b, qf, kk):
    n, c = b.shape
    per_chunk = CHUNK // LOCAL
    b4 = b.reshape(n // CHUNK, per_chunk, LOCAL, c)
    last = b4[:, :per_chunk - 1, LOCAL - 1:LOCAL, :]
    ref = jnp.concatenate([jnp.zeros_like(last[:, :1]), last], axis=1)
    d = (b4 - ref).reshape(n, c)
    return (qf * jnp.exp(d)).astype(BF16), (kk * jnp.exp(-d)).astype(BF16)


def _level_operands(b, fgt, qf, kk, min_half=1):
    n, c = b.shape
    out = []
    for half in LEVELS:
        if half >= min_half and half >= SUBLANES:
            shape4 = (n // (2 * half), 2, half, c)
            b4, q4, k4 = b.reshape(shape4), qf.reshape(shape4), kk.reshape(shape4)
            ref = b4[:, 0, half - 1:half, :]
            z = jnp.stack([k4[:, 0] * jnp.exp(ref - b4[:, 0]),
                           q4[:, 1] * jnp.exp(b4[:, 1] - ref)], axis=1)
            out.append(z.reshape(n, c).astype(BF16))
    if min_half >= SUBLANES:
        return out
    shape3 = (n // SUBLANES, SUBLANES, c)
    j = lax.broadcasted_iota(jnp.int32, (1, SUBLANES, 1), 1)
    b3, f3, q3, k3 = (a.reshape(shape3) for a in (b, fgt, qf, kk))
    sc = jnp.exp(-jnp.abs(b3 - b3[:, 3:4, :]))
    out.append((jnp.where(j >= 4, q3, k3) * sc).reshape(n, c).astype(BF16))
    qff = q3 * f3
    prev = pltpu.roll(f3, 1, 1)
    nxt = pltpu.roll(f3, SUBLANES - 1, 1)
    j4 = j & 3
    z2 = jnp.where(j4 >= 2, qff * jnp.where(j4 == 3, prev, 1.0),
                   k3 * jnp.where(j4 == 0, nxt, 1.0))
    out.append(z2.reshape(n, c).astype(BF16))
    out.append(jnp.where((j & 1) == 1, qff, k3).reshape(n, c).astype(BF16))
    return out


def _mixer_kernel(x_ref, xn_ref, w_in32_ref, pool_w32_ref, pool_scale_ref, lb_ref, gnorm_ref,
                  w_out32_ref, g1_ref, b1_ref, out_ref, w_in_ref, pool_w_ref, w_out_ref,
                  fpre_ref, flag_ref, st_ref, halo_ref, *, tm, alpha, heads):
    ti = pl.program_id(1)
    pw = GROUP * len(POOL_WINDOWS)
    hw = GROUP * heads
    n_chunks = tm // CHUNK

    @pl.when(ti == 0)
    def _():
        st_ref[...] = jnp.zeros_like(st_ref)
        halo_ref[...] = jnp.zeros_like(halo_ref)

    lbr = lb_ref[...]
    mx = jnp.max(lbr, axis=0, keepdims=True)
    e = jnp.exp(lbr - mx)
    lb = e[0:1, :] / jnp.sum(e, axis=0, keepdims=True)
    frames = (lax.broadcasted_iota(jnp.int32, (tm, 1), 0) + ti * tm + 1).astype(F32)
    inv_cnt = [1.0 / jnp.minimum(frames, float(w)) for w in POOL_WINDOWS]
    t_i = lax.broadcasted_iota(jnp.int32, (CHUNK, CHUNK), 0)
    s_i = lax.broadcasted_iota(jnp.int32, (CHUNK, CHUNK), 1)
    masks = [((t_i & half) != 0) & ((s_i & half) == 0)
             & ((t_i // (2 * half)) == (s_i // (2 * half))) for half in LEVELS]
    local_mask = ((t_i // LOCAL) == (s_i // LOCAL)) & (s_i <= t_i)
    gn = gnorm_ref[...]

    f_cols = slice(pw + hw, pw + 2 * hw)
    n_s = x_ref.shape[0]

    def project_gates(src_ref):
        f_pre_min = None
        for si in range(n_s):
            fpre_ref[si] = jnp.dot(src_ref[si].astype(BF16), w_in_ref[:, f_cols],
                                   preferred_element_type=F32)
            m = jnp.min(fpre_ref[si], axis=0, keepdims=True)
            f_pre_min = m if f_pre_min is None else jnp.minimum(f_pre_min, m)
            yield
        fgt_min = jnp.min(lb + (1.0 - lb) * jax.nn.sigmoid(f_pre_min))
        flag_ref[0] = (fgt_min >= MIN_FORGET_LOCAL).astype(jnp.int32)

    @pl.when((pl.program_id(0) == 0) & (ti == 0))
    def _():
        w_in_ref[...] = w_in32_ref[...].astype(BF16)
        pool_w_ref[...] = pool_w32_ref[...].astype(BF16)
        w_out_ref[...] = w_out32_ref[...].astype(BF16)
        _interleave([project_gates(x_ref)])

    local_ok = flag_ref[0] == 1

    def stream(si, fast):
        x = x_ref[si]
        xb = x.astype(BF16)
        left = jnp.dot(xb, w_in_ref[:, 0:pw + hw], preferred_element_type=F32)
        right = jnp.dot(xb, w_in_ref[:, pw + 2 * hw:pw + 4 * hw], preferred_element_type=F32)
        xp = left[:, 0:pw]
        q = left[:, pw:pw + hw]
        f_pre = fpre_ref[si]
        v = right[:, 0:hw]
        g = right[:, hw:2 * hw]
        yield

        ext = jnp.concatenate([halo_ref[si], xp], axis=0)
        halo_ref[si] = xp[tm - POOL_HALO:, :]
        ds = []
        for gi, w in enumerate(POOL_WINDOWS):
            cols = slice(gi * GROUP, (gi + 1) * GROUP)
            s = ext[:, cols]
            width = 1
            while width < w:
                s = s + pltpu.roll(s, width, 0)
                width *= 2
            ds.append((s[POOL_HALO:, :] * inv_cnt[gi] - xp[:, cols]).astype(BF16))
        fgt = lb + (1.0 - lb) * jax.nn.sigmoid(f_pre)
        lf = jnp.log(fgt)
        kk = 1.0 - fgt
        qf = q * jax.nn.sigmoid(q)
        b = _chunk_cumsum(lf)
        yield

        y_a = jnp.concatenate(
            [jnp.dot(ds[gi], pool_w_ref[gi], preferred_element_type=F32)
             for gi in range(len(POOL_WINDOWS))], axis=-1) * pool_scale_ref[...]
        yield

        if fast:
            z = _level_operands(b, fgt, qf, kk, min_half=LOCAL)
            terms = [(zl, zl, masks[li]) for li, zl in enumerate(z)]
            terms.append(_local_operands(b, qf, kk) + (local_mask,))
        else:
            z = _level_operands(b, fgt, qf, kk)
            terms = [(zl, zl, masks[li]) for li, zl in enumerate(z)]
            qk = qf * kk
        qd = (qf * jnp.exp(b)).astype(BF16)
        vb = v.astype(BF16)
        kd, dec = [], []
        for c in range(n_chunks):
            b_c = b[c * CHUNK:(c + 1) * CHUNK, :]
            b_last = b_c[CHUNK - 1:CHUNK, :]
            kd.append((kk[c * CHUNK:(c + 1) * CHUNK, :] * jnp.exp(b_last - b_c)).astype(BF16))
            dec.append(jnp.exp(b_last))
        yield

        p_lv = {}
        upd = {}
        for c in range(n_chunks):
            rows = slice(c * CHUNK, (c + 1) * CHUNK)
            for hd in range(heads):
                cols = slice(hd * GROUP, (hd + 1) * GROUP)
                p_lv[c, hd] = [lax.dot_general(lhs[rows, cols], rhs[rows, cols], NT_DIMS,
                                               preferred_element_type=F32)
                               for lhs, rhs, _ in terms]
                upd[c, hd] = lax.dot_general(vb[rows, cols], kd[c][:, cols], TN_DIMS,
                                             preferred_element_type=F32)
            yield

        o_intra = {}
        for c in range(n_chunks):
            rows = slice(c * CHUNK, (c + 1) * CHUNK)
            for hd in range(heads):
                cols = slice(hd * GROUP, (hd + 1) * GROUP)
                a = jnp.zeros((CHUNK, CHUNK), F32)
                for (_, _, mask), p in zip(terms, p_lv[c, hd]):
                    a = jnp.where(mask, p, a)
                o_intra[c, hd] = jnp.dot(a.astype(BF16), vb[rows, cols],
                                         preferred_element_type=F32)
            yield

        o_inter = {}
        for hd in range(heads):
            cols = slice(hd * GROUP, (hd + 1) * GROUP)
            st = st_ref[si, hd]
            for c in range(n_chunks):
                rows = slice(c * CHUNK, (c + 1) * CHUNK)
                o_inter[c, hd] = lax.dot_general(qd[rows, cols], st.astype(BF16), NT_DIMS,
                                                 preferred_element_type=F32)
                st = dec[c][:, cols] * st + upd[c, hd]
            st_ref[si, hd] = st
        yield

        o_heads = []
        for hd in range(heads):
            cols = slice(hd * GROUP, (hd + 1) * GROUP)
            o_h = jnp.concatenate([o_intra[c, hd] + o_inter[c, hd] for c in range(n_chunks)],
                                  axis=0)
            if not fast:
                diag = jnp.sum(qk[:, cols], axis=-1, keepdims=True)
                o_h = o_h + diag * v[:, cols]
            ms = jnp.mean(o_h * o_h, axis=-1, keepdims=True)
            g_h = g[:, cols]
            o_heads.append(o_h * lax.rsqrt(ms + RMS_EPS) * gn * (g_h * jax.nn.sigmoid(g_h)))
        y = jnp.concatenate([y_a] + o_heads, axis=-1).astype(BF16)
        yield

        mix = jnp.dot(y, w_out_ref[...], preferred_element_type=F32)
        yield
        out_ref[si] = _layer_norm(alpha * x + mix, g1_ref[...], b1_ref[...])

    def body(fast):
        _interleave([stream(si, fast) for si in range(n_s)],
                    fill=[project_gates(xn_ref)], plan=GATES_AHEAD_PLAN)

    pl.when(local_ok)(functools.partial(body, True))
    pl.when(jnp.logical_not(local_ok))(functools.partial(body, False))


def _gelu_tanh(a):
    c = 0.7978845608028654
    return 0.5 * a * (1.0 + jnp.tanh(c * (a + 0.044715 * (a * a * a))))


def _ffn_kernel(h_ref, w_up_ref, conv_w_ref, conv_b_ref, w_down_ref, g2_ref, b2_ref,
                out_ref, halo_ref, pin_ref, pout_ref, gated_ref, *, tm, alpha, d_ff):
    ti = pl.program_id(1)
    r = tm // SUBLANES
    n_cb = h_ref.shape[2] // GROUP
    n_chunks = d_ff // FF_CHUNK
    chunks = [(slice(j * FF_CHUNK, (j + 1) * FF_CHUNK),
               slice(d_ff + j * FF_CHUNK, d_ff + (j + 1) * FF_CHUNK)) for j in range(n_chunks)]

    @pl.when(ti == 0)
    def _():
        halo_ref[...] = jnp.zeros_like(halo_ref)

    def stream(si):
        for c in range(n_cb):
            pin_ref[si, c] = h_ref[si, :, c * GROUP:(c + 1) * GROUP]
        h = jnp.concatenate(
            [jnp.concatenate([pin_ref[si, c, pl.ds(s, r, stride=SUBLANES), :]
                              for c in range(n_cb)], axis=1) for s in range(SUBLANES)], axis=0)
        hb = h.astype(BF16)

        def up(cols):
            return jnp.dot(hb, w_up_ref[:, cols], preferred_element_type=F32)

        def conv(u, cols):
            shifted = []
            for k in range(CONV_WIDTH - 1):
                blk = u[(6 + k) * r:(7 + k) * r, :]
                ext = jnp.concatenate([halo_ref[si, k, :, cols], blk], axis=0)
                halo_ref[si, k, :, cols] = blk[r - CONV_HALO:, :]
                shifted.append(pltpu.roll(ext, 1, 0)[CONV_HALO:, :])
            u1 = jnp.concatenate([shifted[1], u[:7 * r, :]], axis=0)
            u2 = jnp.concatenate(shifted + [u[:6 * r, :]], axis=0)
            cw = conv_w_ref[:, cols]
            return conv_b_ref[:, cols] + cw[0:1, :] * u2 + cw[1:2, :] * u1 + cw[2:3, :] * u

        for ca, cv in chunks:
            gated_ref[si, :, ca] = (_gelu_tanh(conv(up(ca), ca)) * conv(up(cv), cv)).astype(BF16)
        yield
        acc = jnp.dot(gated_ref[si], w_down_ref[...], preferred_element_type=F32)
        res = _layer_norm(alpha * h + acc, g2_ref[...], b2_ref[...])
        for c in range(n_cb):
            for s in range(SUBLANES):
                pout_ref[si, c, pl.ds(s, r, stride=SUBLANES), :] = (
                    res[s * r:(s + 1) * r, c * GROUP:(c + 1) * GROUP])
        for c in range(n_cb):
            out_ref[si, :, c * GROUP:(c + 1) * GROUP] = pout_ref[si, c]

    _interleave([stream(si) for si in range(h_ref.shape[0])])


def _const_spec(shape):
    nd = len(shape)
    return pl.BlockSpec(shape, lambda b, t: (0,) * nd, pipeline_mode=pl.Buffered(1))


def _nbytes(shape, dtype):
    return math.prod(shape) * jnp.dtype(dtype).itemsize


def _vmem_limit(consts, blocks, scratch):
    total = (sum(_nbytes(c.shape, c.dtype) for c in consts)
             + 2 * sum(_nbytes(*b) for b in blocks)
             + sum(_nbytes(s.shape, s.dtype) for s in scratch)
             + TEMPORARIES_VMEM_BYTES)
    assert total <= V7X_VMEM_BYTES, "tile plan does not fit VMEM"
    return total


def _mixer_ln1(x, w_in, pool_w, pool_scale, lb, gnorm, w_out, g1, b1, *, alpha):
    bsz, seq, d = x.shape
    tm = MIXER_TILE
    heads = lb.shape[1] // GROUP
    kernel = functools.partial(_mixer_kernel, tm=tm, alpha=alpha, heads=heads)
    consts = (w_in, pool_w, pool_scale, lb, gnorm, w_out, g1, b1)
    n_t = seq // tm
    n_tiles = (bsz // STREAMS) * n_t

    def next_tile(b, t):
        g = jnp.minimum(b * n_t + t + 1, n_tiles - 1)
        return (g // n_t, g % n_t, 0)

    block = (STREAMS, tm, d)
    scratch = [
        pltpu.VMEM(w_in.shape, BF16), pltpu.VMEM(pool_w.shape, BF16), pltpu.VMEM(w_out.shape, BF16),
        pltpu.VMEM((STREAMS, tm, lb.shape[1]), F32),
        pltpu.SMEM((1,), jnp.int32),
        pltpu.VMEM((STREAMS, heads, GROUP, GROUP), F32),
        pltpu.VMEM((STREAMS, POOL_HALO, GROUP * len(POOL_WINDOWS)), F32),
    ]
    return pl.pallas_call(
        kernel,
        grid=(bsz // STREAMS, n_t),
        in_specs=[pl.BlockSpec(block, lambda b, t: (b, t, 0)),
                  pl.BlockSpec(block, next_tile)]
        + [_const_spec(c.shape) for c in consts],
        out_specs=pl.BlockSpec(block, lambda b, t: (b, t, 0)),
        out_shape=jax.ShapeDtypeStruct(x.shape, F32),
        scratch_shapes=scratch,
        compiler_params=pltpu.CompilerParams(
            dimension_semantics=("arbitrary", "arbitrary"),
            vmem_limit_bytes=_vmem_limit(consts, [(block, F32)] * 3, scratch)),
        name="mixer_ln1",
    )(x, x, *consts)


def _ffn_ln2(h, w_up, conv_w, conv_b, w_down, g2, b2, *, alpha):
    bsz, seq, d = h.shape
    tm = FFN_TILE
    d_ff = w_down.shape[0]
    kernel = functools.partial(_ffn_kernel, tm=tm, alpha=alpha, d_ff=d_ff)
    consts = (w_up, conv_w, conv_b, w_down, g2, b2)
    ns = FFN_STREAMS
    block = (ns, tm, d)
    scratch = [
        pltpu.VMEM((ns, CONV_WIDTH - 1, CONV_HALO, 2 * d_ff), F32),
        pltpu.VMEM((ns, d // GROUP, tm, GROUP), F32),
        pltpu.VMEM((ns, d // GROUP, tm, GROUP), F32),
        pltpu.VMEM((ns, tm, d_ff), BF16),
    ]
    return pl.pallas_call(
        kernel,
        grid=(bsz // ns, seq // tm),
        in_specs=[pl.BlockSpec(block, lambda b, t: (b, t, 0))]
        + [_const_spec(c.shape) for c in consts],
        out_specs=pl.BlockSpec(block, lambda b, t: (b, t, 0)),
        out_shape=jax.ShapeDtypeStruct(h.shape, F32),
        scratch_shapes=scratch,
        compiler_params=pltpu.CompilerParams(
            dimension_semantics=("arbitrary", "arbitrary"),
            vmem_limit_bytes=_vmem_limit(consts, [(block, F32)] * 2, scratch)),
        name="ffn_ln2",
    )(h, *consts)


def kernel(x, w_in, pool_w, pool_scale, hgrn_lb, hgrn_gnorm, w_out, ln1_g, ln1_b,
           w_up, conv_w, conv_b, w_down, ln2_g, ln2_b):
    depth = w_in.shape[0]
    assert depth == 1 and hgrn_lb.shape[0] == 2, "single-layer block expected"
    assert w_down.shape[1] % FF_CHUNK == 0
    assert x.shape[0] % STREAMS == 0 and x.shape[0] % FFN_STREAMS == 0
    assert x.shape[1] % MIXER_TILE == 0 and x.shape[1] % FFN_TILE == 0
    assert MIXER_TILE % CHUNK == 0 and FFN_TILE % (SUBLANES * CONV_HALO) == 0
    alpha = (2.0 * depth) ** 0.25
    h = _mixer_ln1(
        x, w_in[0], pool_w[0], pool_scale, hgrn_lb, hgrn_gnorm, w_out[0], ln1_g, ln1_b,
        alpha=alpha)
    return _ffn_ln2(
        h, w_up[0].astype(BF16), conv_w[0], conv_b, w_down[0].astype(BF16),
        ln2_g, ln2_b, alpha=alpha)
```

```python
import functools
import math

import jax
import jax.numpy as jnp
from jax import lax
from jax.experimental import pallas as pl
from jax.experimental.pallas import tpu as pltpu

F32 = jnp.float32
BF16 = jnp.bfloat16

V7X_VMEM_BYTES = 64 * 2 ** 20
TEMPORARIES_VMEM_BYTES = 24 * 2 ** 20
STREAMS = 2
FFN_STREAMS = 2
MIXER_TILE = 256
FFN_TILE = 256
CHUNK = 64
POOL_WINDOWS = (2, 4, 8, 16)
GROUP = 128
SUBLANES = 8
POOL_HALO = 16
CONV_WIDTH = 3
CONV_HALO = 8
FF_CHUNK = 256
LN_EPS = 1e-5
RMS_EPS = 1e-6
LEVELS = (32, 16, 8, 4, 2, 1)
LOCAL = 16
MAX_LOCAL_EXPONENT = 80.0
MIN_FORGET_LOCAL = math.exp(-MAX_LOCAL_EXPONENT / LOCAL)
GATES_AHEAD_PLAN = {4: 1, 6: 1}

NT_DIMS = (((1,), (1,)), ((), ()))
TN_DIMS = (((0,), (0,)), ((), ()))


def _advance(gens):
    while gens:
        g = gens.pop(0)
        try:
            next(g)
        except StopIteration:
            continue
        gens.append(g)
        return True
    return False


def _interleave(streams, fill=(), plan=None):
    live, fill = list(streams), list(fill)
    plan = plan or {}
    n = 0
    while True:
        for _ in range(plan.get(n, 0)):
            _advance(fill)
        if not _advance(live):
            break
        n += 1
    while _advance(fill):
        pass


def _layer_norm(z, g, b):
    mu = jnp.mean(z, axis=-1, keepdims=True)
    zc = z - mu
    var = jnp.mean(zc * zc, axis=-1, keepdims=True)
    return zc * lax.rsqrt(var + LN_EPS) * g + b


def _chunk_cumsum(a):
    n, c = a.shape
    a3 = a.reshape(n // SUBLANES, SUBLANES, c)
    j = lax.broadcasted_iota(jnp.int32, (1, SUBLANES, 1), 1)
    for k in (1, 2, 4):
        a3 = a3 + jnp.where(j >= k, pltpu.roll(a3, k, 1), 0.0)
    a4 = a3.reshape(n // CHUNK, CHUNK // SUBLANES, SUBLANES, c)
    carry = None
    groups = []
    for i in range(CHUNK // SUBLANES):
        g = a4[:, i] if carry is None else a4[:, i] + carry
        carry = g[:, SUBLANES - 1:SUBLANES, :]
        groups.append(g)
    return jnp.stack(groups, axis=1).reshape(n, c)


def _local_operands(b, qf, kk):
    n, c = b.shape
    per_chunk = CHUNK // LOCAL
    b4 = b.reshape(n // CHUNK, per_chunk, LOCAL, c)
    last = b4[:, :per_chunk - 1, LOCAL - 1:LOCAL, :]
    ref = jnp.concatenate([jnp.zeros_like(last[:, :1]), last], axis=1)
    d = (b4 - ref).reshape(n, c)
    return (qf * jnp.exp(d)).astype(BF16), (kk * jnp.exp(-d)).astype(BF16)


def _slot_operands(zs, halves):
    lhs, rhs = [], []
    for z, half in zip(zs, halves):
        zero = jnp.zeros_like(z)

        def only_rows(r0, r1, z=z, zero=zero):
            parts = ([zero[:r0]] if r0 else []) + [z[r0:r1]] + ([zero[r1:]] if r1 < CHUNK else [])
            return jnp.concatenate(parts, axis=0)

        for lo in range(0, CHUNK, 2 * half):
            lhs.append(only_rows(lo + half, lo + 2 * half))
            rhs.append(only_rows(lo, lo + half))
    return jnp.concatenate(lhs, axis=1), jnp.concatenate(rhs, axis=1)


def _level_operands(b, fgt, qf, kk, min_half=1):
    n, c = b.shape
    out = []
    for half in LEVELS:
        if half >= min_half and half >= SUBLANES:
            shape4 = (n // (2 * half), 2, half, c)
            b4, q4, k4 = b.reshape(shape4), qf.reshape(shape4), kk.reshape(shape4)
            ref = b4[:, 0, half - 1:half, :]
            z = jnp.stack([k4[:, 0] * jnp.exp(ref - b4[:, 0]),
                           q4[:, 1] * jnp.exp(b4[:, 1] - ref)], axis=1)
            out.append(z.reshape(n, c).astype(BF16))
    if min_half >= SUBLANES:
        return out
    shape3 = (n // SUBLANES, SUBLANES, c)
    j = lax.broadcasted_iota(jnp.int32, (1, SUBLANES, 1), 1)
    b3, f3, q3, k3 = (a.reshape(shape3) for a in (b, fgt, qf, kk))
    sc = jnp.exp(-jnp.abs(b3 - b3[:, 3:4, :]))
    out.append((jnp.where(j >= 4, q3, k3) * sc).reshape(n, c).astype(BF16))
    qff = q3 * f3
    prev = pltpu.roll(f3, 1, 1)
    nxt = pltpu.roll(f3, SUBLANES - 1, 1)
    j4 = j & 3
    z2 = jnp.where(j4 >= 2, qff * jnp.where(j4 == 3, prev, 1.0),
                   k3 * jnp.where(j4 == 0, nxt, 1.0))
    out.append(z2.reshape(n, c).astype(BF16))
    out.append(jnp.where((j & 1) == 1, qff, k3).reshape(n, c).astype(BF16))
    return out


def _mixer_kernel(x_ref, xn_ref, w_in_ref, pool_w_ref, pool_scale_ref, lb_ref, gnorm_ref,
                  w_out_ref, g1_ref, b1_ref, out_ref, fpre_ref, flag_ref, st_ref, halo_ref,
                  *, tm, alpha, heads):
    ti = pl.program_id(1)
    pw = GROUP * len(POOL_WINDOWS)
    hw = GROUP * heads
    n_chunks = tm // CHUNK

    @pl.when(ti == 0)
    def _():
        st_ref[...] = jnp.zeros_like(st_ref)
        halo_ref[...] = jnp.zeros_like(halo_ref)

    lbr = lb_ref[...]
    mx = jnp.max(lbr, axis=0, keepdims=True)
    e = jnp.exp(lbr - mx)
    lb = e[0:1, :] / jnp.sum(e, axis=0, keepdims=True)
    frames = (lax.broadcasted_iota(jnp.int32, (tm, 1), 0) + ti * tm + 1).astype(F32)
    inv_cnt = [1.0 / jnp.minimum(frames, float(w)) for w in POOL_WINDOWS]
    t_i = lax.broadcasted_iota(jnp.int32, (CHUNK, CHUNK), 0)
    s_i = lax.broadcasted_iota(jnp.int32, (CHUNK, CHUNK), 1)
    masks = [((t_i & half) != 0) & ((s_i & half) == 0)
             & ((t_i // (2 * half)) == (s_i // (2 * half))) for half in LEVELS]
    local_mask = ((t_i // LOCAL) == (s_i // LOCAL)) & (s_i <= t_i)
    gn = gnorm_ref[...]

    f_cols = slice(pw + hw, pw + 2 * hw)
    n_s = x_ref.shape[0]

    def project_gates(src_ref):
        f_pre_min = None
        for si in range(n_s):
            fpre_ref[si] = jnp.dot(src_ref[si].astype(BF16), w_in_ref[:, f_cols],
                                   preferred_element_type=F32)
            m = jnp.min(fpre_ref[si], axis=0, keepdims=True)
            f_pre_min = m if f_pre_min is None else jnp.minimum(f_pre_min, m)
            yield
        fgt_min = jnp.min(lb + (1.0 - lb) * jax.nn.sigmoid(f_pre_min))
        flag_ref[0] = (fgt_min >= MIN_FORGET_LOCAL).astype(jnp.int32)

    @pl.when((pl.program_id(0) == 0) & (ti == 0))
    def _():
        _interleave([project_gates(x_ref)])

    local_ok = flag_ref[0] == 1

    def stream(si, fast):
        x = x_ref[si]
        xb = x.astype(BF16)
        left = jnp.dot(xb, w_in_ref[:, 0:pw + hw], preferred_element_type=F32)
        right = jnp.dot(xb, w_in_ref[:, pw + 2 * hw:pw + 4 * hw], preferred_element_type=F32)
        xp = left[:, 0:pw]
        q = left[:, pw:pw + hw]
        f_pre = fpre_ref[si]
        v = right[:, 0:hw]
        g = right[:, hw:2 * hw]
        yield

        ext = jnp.concatenate([halo_ref[si], xp], axis=0)
        halo_ref[si] = xp[tm - POOL_HALO:, :]
        ds = []
        for gi, w in enumerate(POOL_WINDOWS):
            cols = slice(gi * GROUP, (gi + 1) * GROUP)
            s = ext[:, cols]
            width = 1
            while width < w:
                s = s + pltpu.roll(s, width, 0)
                width *= 2
            ds.append((s[POOL_HALO:, :] * inv_cnt[gi] - xp[:, cols]).astype(BF16))
        fgt = lb + (1.0 - lb) * jax.nn.sigmoid(f_pre)
        lf = jnp.log(fgt)
        kk = 1.0 - fgt
        qf = q * jax.nn.sigmoid(q)
        b = _chunk_cumsum(lf)
        yield

        y_a = jnp.concatenate(
            [jnp.dot(ds[gi], pool_w_ref[gi], preferred_element_type=F32)
             for gi in range(len(POOL_WINDOWS))], axis=-1) * pool_scale_ref[...]
        yield

        if fast:
            z = _level_operands(b, fgt, qf, kk, min_half=LOCAL)
            q_loc, k_loc = _local_operands(b, qf, kk)
        else:
            z = _level_operands(b, fgt, qf, kk)
            terms = [(zl, zl, masks[li]) for li, zl in enumerate(z)]
            qk = qf * kk
        qd = (qf * jnp.exp(b)).astype(BF16)
        vb = v.astype(BF16)
        kd, dec = [], []
        for c in range(n_chunks):
            b_c = b[c * CHUNK:(c + 1) * CHUNK, :]
            b_last = b_c[CHUNK - 1:CHUNK, :]
            kd.append((kk[c * CHUNK:(c + 1) * CHUNK, :] * jnp.exp(b_last - b_c)).astype(BF16))
            dec.append(jnp.exp(b_last))
        yield

        p_lv = {}
        upd = {}
        for c in range(n_chunks):
            rows = slice(c * CHUNK, (c + 1) * CHUNK)
            for hd in range(heads):
                cols = slice(hd * GROUP, (hd + 1) * GROUP)
                if fast:
                    lhs, rhs = _slot_operands([zl[rows, cols] for zl in z], LEVELS[:len(z)])
                    pairs = [(lhs, rhs), (q_loc[rows, cols], k_loc[rows, cols])]
                else:
                    pairs = [(lhs[rows, cols], rhs[rows, cols]) for lhs, rhs, _ in terms]
                p_lv[c, hd] = [lax.dot_general(lhs, rhs, NT_DIMS, preferred_element_type=F32)
                               for lhs, rhs in pairs]
                upd[c, hd] = lax.dot_general(vb[rows, cols], kd[c][:, cols], TN_DIMS,
                                             preferred_element_type=F32)
            yield

        o_intra = {}
        for c in range(n_chunks):
            rows = slice(c * CHUNK, (c + 1) * CHUNK)
            for hd in range(heads):
                cols = slice(hd * GROUP, (hd + 1) * GROUP)
                if fast:
                    a = jnp.where(local_mask, p_lv[c, hd][1], p_lv[c, hd][0])
                else:
                    a = jnp.zeros((CHUNK, CHUNK), F32)
                    for (_, _, mask), p in zip(terms, p_lv[c, hd]):
                        a = jnp.where(mask, p, a)
                o_intra[c, hd] = jnp.dot(a.astype(BF16), vb[rows, cols],
                                         preferred_element_type=F32)
            yield

        o_inter = {}
        for hd in range(heads):
            cols = slice(hd * GROUP, (hd + 1) * GROUP)
            st = st_ref[si, hd]
            for c in range(n_chunks):
                rows = slice(c * CHUNK, (c + 1) * CHUNK)
                o_inter[c, hd] = lax.dot_general(qd[rows, cols], st.astype(BF16), NT_DIMS,
                                                 preferred_element_type=F32)
                st = dec[c][:, cols] * st + upd[c, hd]
            st_ref[si, hd] = st
        yield

        o_heads = []
        for hd in range(heads):
            cols = slice(hd * GROUP, (hd + 1) * GROUP)
            o_h = jnp.concatenate([o_intra[c, hd] + o_inter[c, hd] for c in range(n_chunks)],
                                  axis=0)
            if not fast:
                diag = jnp.sum(qk[:, cols], axis=-1, keepdims=True)
                o_h = o_h + diag * v[:, cols]
            ms = jnp.mean(o_h * o_h, axis=-1, keepdims=True)
            g_h = g[:, cols]
            o_heads.append(o_h * lax.rsqrt(ms + RMS_EPS) * gn * (g_h * jax.nn.sigmoid(g_h)))
        y = jnp.concatenate([y_a] + o_heads, axis=-1).astype(BF16)
        yield

        mix = jnp.dot(y, w_out_ref[...], preferred_element_type=F32)
        yield
        out_ref[si] = _layer_norm(alpha * x + mix, g1_ref[...], b1_ref[...])

    def body(fast):
        _interleave([stream(si, fast) for si in range(n_s)],
                    fill=[project_gates(xn_ref)], plan=GATES_AHEAD_PLAN)

    pl.when(local_ok)(functools.partial(body, True))
    pl.when(jnp.logical_not(local_ok))(functools.partial(body, False))


def _gelu_tanh(a):
    c = 0.7978845608028654
    return 0.5 * a * (1.0 + jnp.tanh(c * (a + 0.044715 * (a * a * a))))


def _ffn_kernel(h_ref, w_up_ref, conv_w_ref, conv_b_ref, w_down_ref, g2_ref, b2_ref,
                out_ref, halo_ref, pin_ref, pout_ref, gated_ref, *, tm, alpha, d_ff):
    ti = pl.program_id(1)
    r = tm // SUBLANES
    n_cb = h_ref.shape[2] // GROUP
    n_chunks = d_ff // FF_CHUNK
    chunks = [(slice(j * FF_CHUNK, (j + 1) * FF_CHUNK),
               slice(d_ff + j * FF_CHUNK, d_ff + (j + 1) * FF_CHUNK)) for j in range(n_chunks)]

    @pl.when(ti == 0)
    def _():
        halo_ref[...] = jnp.zeros_like(halo_ref)

    def stream(si):
        for c in range(n_cb):
            pin_ref[si, c] = h_ref[si, :, c * GROUP:(c + 1) * GROUP]
        h = jnp.concatenate(
            [jnp.concatenate([pin_ref[si, c, pl.ds(s, r, stride=SUBLANES), :]
                              for c in range(n_cb)], axis=1) for s in range(SUBLANES)], axis=0)
        hb = h.astype(BF16)

        def up(cols):
            return jnp.dot(hb, w_up_ref[:, cols], preferred_element_type=F32)

        def conv(u, cols):
            shifted = []
            for k in range(CONV_WIDTH - 1):
                blk = u[(6 + k) * r:(7 + k) * r, :]
                ext = jnp.concatenate([halo_ref[si, k, :, cols], blk], axis=0)
                halo_ref[si, k, :, cols] = blk[r - CONV_HALO:, :]
                shifted.append(pltpu.roll(ext, 1, 0)[CONV_HALO:, :])
            u1 = jnp.concatenate([shifted[1], u[:7 * r, :]], axis=0)
            u2 = jnp.concatenate(shifted + [u[:6 * r, :]], axis=0)
            cw = conv_w_ref[:, cols]
            return conv_b_ref[:, cols] + cw[0:1, :] * u2 + cw[1:2, :] * u1 + cw[2:3, :] * u

        for ca, cv in chunks:
            gated_ref[si, :, ca] = (_gelu_tanh(conv(up(ca), ca)) * conv(up(cv), cv)).astype(BF16)
        yield
        acc = jnp.dot(gated_ref[si], w_down_ref[...], preferred_element_type=F32)
        res = _layer_norm(alpha * h + acc, g2_ref[...], b2_ref[...])
        for c in range(n_cb):
            for s in range(SUBLANES):
                pout_ref[si, c, pl.ds(s, r, stride=SUBLANES), :] = (
                    res[s * r:(s + 1) * r, c * GROUP:(c + 1) * GROUP])
        for c in range(n_cb):
            out_ref[si, :, c * GROUP:(c + 1) * GROUP] = pout_ref[si, c]

    _interleave([stream(si) for si in range(h_ref.shape[0])])


def _const_spec(shape):
    nd = len(shape)
    return pl.BlockSpec(shape, lambda b, t: (0,) * nd, pipeline_mode=pl.Buffered(1))


def _nbytes(shape, dtype):
    return math.prod(shape) * jnp.dtype(dtype).itemsize


def _vmem_limit(consts, blocks, scratch):
    total = (sum(_nbytes(c.shape, c.dtype) for c in consts)
             + 2 * sum(_nbytes(*b) for b in blocks)
             + sum(_nbytes(s.shape, s.dtype) for s in scratch)
             + TEMPORARIES_VMEM_BYTES)
    assert total <= V7X_VMEM_BYTES, "tile plan does not fit VMEM"
    return total


def _mixer_ln1(x, w_in, pool_w, pool_scale, lb, gnorm, w_out, g1, b1, *, alpha):
    bsz, seq, d = x.shape
    tm = MIXER_TILE
    heads = lb.shape[1] // GROUP
    kernel = functools.partial(_mixer_kernel, tm=tm, alpha=alpha, heads=heads)
    consts = (w_in, pool_w, pool_scale, lb, gnorm, w_out, g1, b1)
    n_t = seq // tm
    n_tiles = (bsz // STREAMS) * n_t

    def next_tile(b, t):
        g = jnp.minimum(b * n_t + t + 1, n_tiles - 1)
        return (g // n_t, g % n_t, 0)

    block = (STREAMS, tm, d)
    scratch = [
        pltpu.VMEM((STREAMS, tm, lb.shape[1]), F32),
        pltpu.SMEM((1,), jnp.int32),
        pltpu.VMEM((STREAMS, heads, GROUP, GROUP), F32),
        pltpu.VMEM((STREAMS, POOL_HALO, GROUP * len(POOL_WINDOWS)), F32),
    ]
    return pl.pallas_call(
        kernel,
        grid=(bsz // STREAMS, n_t),
        in_specs=[pl.BlockSpec(block, lambda b, t: (b, t, 0)),
                  pl.BlockSpec(block, next_tile)]
        + [_const_spec(c.shape) for c in consts],
        out_specs=pl.BlockSpec(block, lambda b, t: (b, t, 0)),
        out_shape=jax.ShapeDtypeStruct(x.shape, F32),
        scratch_shapes=scratch,
        compiler_params=pltpu.CompilerParams(
            dimension_semantics=("arbitrary", "arbitrary"),
            vmem_limit_bytes=_vmem_limit(consts, [(block, F32)] * 3, scratch)),
        name="mixer_ln1",
    )(x, x, *consts)


def _ffn_ln2(h, w_up, conv_w, conv_b, w_down, g2, b2, *, alpha):
    bsz, seq, d = h.shape
    tm = FFN_TILE
    d_ff = w_down.shape[0]
    kernel = functools.partial(_ffn_kernel, tm=tm, alpha=alpha, d_ff=d_ff)
    consts = (w_up, conv_w, conv_b, w_down, g2, b2)
    ns = FFN_STREAMS
    block = (ns, tm, d)
    scratch = [
        pltpu.VMEM((ns, CONV_WIDTH - 1, CONV_HALO, 2 * d_ff), F32),
        pltpu.VMEM((ns, d // GROUP, tm, GROUP), F32),
        pltpu.VMEM((ns, d // GROUP, tm, GROUP), F32),
        pltpu.VMEM((ns, tm, d_ff), BF16),
    ]
    return pl.pallas_call(
        kernel,
        grid=(bsz // ns, seq // tm),
        in_specs=[pl.BlockSpec(block, lambda b, t: (b, t, 0))]
        + [_const_spec(c.shape) for c in consts],
        out_specs=pl.BlockSpec(block, lambda b, t: (b, t, 0)),
        out_shape=jax.ShapeDtypeStruct(h.shape, F32),
        scratch_shapes=scratch,
        compiler_params=pltpu.CompilerParams(
            dimension_semantics=("arbitrary", "arbitrary"),
            vmem_limit_bytes=_vmem_limit(consts, [(block, F32)] * 2, scratch)),
        name="ffn_ln2",
    )(h, *consts)


def kernel(x, w_in, pool_w, pool_scale, hgrn_lb, hgrn_gnorm, w_out, ln1_g, ln1_b,
           w_up, conv_w, conv_b, w_down, ln2_g, ln2_b):
    depth = w_in.shape[0]
    assert depth == 1 and hgrn_lb.shape[0] == 2, "single-layer block expected"
    assert w_down.shape[1] % FF_CHUNK == 0
    assert x.shape[0] % STREAMS == 0 and x.shape[0] % FFN_STREAMS == 0
    assert x.shape[1] % MIXER_TILE == 0 and x.shape[1] % FFN_TILE == 0
    assert MIXER_TILE % CHUNK == 0 and FFN_TILE % (SUBLANES * CONV_HALO) == 0
    alpha = (2.0 * depth) ** 0.25
    h = _mixer_ln1(
        x, w_in[0].astype(BF16), pool_w[0].astype(BF16), pool_scale, hgrn_lb,
        hgrn_gnorm, w_out[0].astype(BF16), ln1_g, ln1_b, alpha=alpha)
    return _ffn_ln2(
        h, w_up[0].astype(BF16), conv_w[0], conv_b, w_down[0].astype(BF16),
        ln2_g, ln2_b, alpha=alpha)
```

```python
import functools
import math

import jax
import jax.numpy as jnp
from jax import lax
from jax.experimental import pallas as pl
from jax.experimental.pallas import tpu as pltpu

F32 = jnp.float32
BF16 = jnp.bfloat16

V7X_VMEM_BYTES = 64 * 2 ** 20
TEMPORARIES_VMEM_BYTES = 24 * 2 ** 20
STREAMS = 2
FFN_STREAMS = 2
MIXER_TILE = 256
FFN_TILE = 256
CHUNK = 64
POOL_WINDOWS = (2, 4, 8, 16)
GROUP = 128
SUBLANES = 8
POOL_HALO = 16
CONV_WIDTH = 3
CONV_HALO = 8
FF_CHUNK = 256
LN_EPS = 1e-5
RMS_EPS = 1e-6
LEVELS = (32, 16, 8, 4, 2, 1)
LOCAL = 16
MAX_LOCAL_EXPONENT = 80.0
MIN_FORGET_LOCAL = math.exp(-MAX_LOCAL_EXPONENT / LOCAL)
GATES_AHEAD_PLAN = {4: 1, 6: 1}

NT_DIMS = (((1,), (1,)), ((), ()))
TN_DIMS = (((0,), (0,)), ((), ()))


def _advance(gens):
    while gens:
        g = gens.pop(0)
        try:
            next(g)
        except StopIteration:
            continue
        gens.append(g)
        return True
    return False


def _interleave(streams, fill=(), plan=None):
    live, fill = list(streams), list(fill)
    plan = plan or {}
    n = 0
    while True:
        for _ in range(plan.get(n, 0)):
            _advance(fill)
        if not _advance(live):
            break
        n += 1
    while _advance(fill):
        pass


def _layer_norm(z, g, b):
    mu = jnp.mean(z, axis=-1, keepdims=True)
    zc = z - mu
    var = jnp.mean(zc * zc, axis=-1, keepdims=True)
    return zc * lax.rsqrt(var + LN_EPS) * g + b


def _chunk_cumsum(a):
    n, c = a.shape
    a3 = a.reshape(n // SUBLANES, SUBLANES, c)
    j = lax.broadcasted_iota(jnp.int32, (1, SUBLANES, 1), 1)
    for k in (1, 2, 4):
        a3 = a3 + jnp.where(j >= k, pltpu.roll(a3, k, 1), 0.0)
    a4 = a3.reshape(n // CHUNK, CHUNK // SUBLANES, SUBLANES, c)
    carry = None
    groups = []
    for i in range(CHUNK // SUBLANES):
        g = a4[:, i] if carry is None else a4[:, i] + carry
        carry = g[:, SUBLANES - 1:SUBLANES, :]
        groups.append(g)
    return jnp.stack(groups, axis=1).reshape(n, c)


def _local_operands(b, qf, kk):
    n, c = b.shape
    per_chunk = CHUNK // LOCAL
    b4 = b.reshape(n // CHUNK, per_chunk, LOCAL, c)
    last = b4[:, :per_chunk - 1, LOCAL - 1:LOCAL, :]
    ref = jnp.concatenate([jnp.zeros_like(last[:, :1]), last], axis=1)
    d = (b4 - ref).reshape(n, c)
    return (qf * jnp.exp(d)).astype(BF16), (kk * jnp.exp(-d)).astype(BF16)


def _slot_operands(zs, halves):
    lhs, rhs = [], []
    for z, half in zip(zs, halves):
        zero = jnp.zeros_like(z)

        def only_rows(r0, r1, z=z, zero=zero):
            parts = ([zero[:r0]] if r0 else []) + [z[r0:r1]] + ([zero[r1:]] if r1 < CHUNK else [])
            return jnp.concatenate(parts, axis=0)

        for lo in range(0, CHUNK, 2 * half):
            lhs.append(only_rows(lo + half, lo + 2 * half))
            rhs.append(only_rows(lo, lo + half))
    return jnp.concatenate(lhs, axis=1), jnp.concatenate(rhs, axis=1)


def _level_operands(b, fgt, qf, kk, min_half=1):
    n, c = b.shape
    out = []
    for half in LEVELS:
        if half >= min_half and half >= SUBLANES:
            shape4 = (n // (2 * half), 2, half, c)
            b4, q4, k4 = b.reshape(shape4), qf.reshape(shape4), kk.reshape(shape4)
            ref = b4[:, 0, half - 1:half, :]
            z = jnp.stack([k4[:, 0] * jnp.exp(ref - b4[:, 0]),
                           q4[:, 1] * jnp.exp(b4[:, 1] - ref)], axis=1)
            out.append(z.reshape(n, c).astype(BF16))
    if min_half >= SUBLANES:
        return out
    shape3 = (n // SUBLANES, SUBLANES, c)
    j = lax.broadcasted_iota(jnp.int32, (1, SUBLANES, 1), 1)
    b3, f3, q3, k3 = (a.reshape(shape3) for a in (b, fgt, qf, kk))
    sc = jnp.exp(-jnp.abs(b3 - b3[:, 3:4, :]))
    out.append((jnp.where(j >= 4, q3, k3) * sc).reshape(n, c).astype(BF16))
    qff = q3 * f3
    prev = pltpu.roll(f3, 1, 1)
    nxt = pltpu.roll(f3, SUBLANES - 1, 1)
    j4 = j & 3
    z2 = jnp.where(j4 >= 2, qff * jnp.where(j4 == 3, prev, 1.0),
                   k3 * jnp.where(j4 == 0, nxt, 1.0))
    out.append(z2.reshape(n, c).astype(BF16))
    out.append(jnp.where((j & 1) == 1, qff, k3).reshape(n, c).astype(BF16))
    return out


def _mixer_kernel(x_ref, xn_ref, w_in_ref, pool_w_ref, pool_scale_ref, lb_ref, gnorm_ref,
                  w_out_ref, g1_ref, b1_ref, out_ref, fpre_ref, flag_ref, st_ref, halo_ref,
                  *, tm, alpha, heads):
    ti = pl.program_id(1)
    pw = GROUP * len(POOL_WINDOWS)
    hw = GROUP * heads
    n_chunks = tm // CHUNK

    @pl.when(ti == 0)
    def _():
        st_ref[...] = jnp.zeros_like(st_ref)
        halo_ref[...] = jnp.zeros_like(halo_ref)

    lbr = lb_ref[...]
    mx = jnp.max(lbr, axis=0, keepdims=True)
    e = jnp.exp(lbr - mx)
    lb = e[0:1, :] / jnp.sum(e, axis=0, keepdims=True)
    frames = (lax.broadcasted_iota(jnp.int32, (tm, 1), 0) + ti * tm + 1).astype(F32)
    inv_cnt = [1.0 / jnp.minimum(frames, float(w)) for w in POOL_WINDOWS]
    t_i = lax.broadcasted_iota(jnp.int32, (CHUNK, CHUNK), 0)
    s_i = lax.broadcasted_iota(jnp.int32, (CHUNK, CHUNK), 1)
    masks = [((t_i & half) != 0) & ((s_i & half) == 0)
             & ((t_i // (2 * half)) == (s_i // (2 * half))) for half in LEVELS]
    local_mask = ((t_i // LOCAL) == (s_i // LOCAL)) & (s_i <= t_i)
    gn = gnorm_ref[...]

    f_cols = slice(pw + hw, pw + 2 * hw)
    n_s = x_ref.shape[0]

    def project_gates(src_ref):
        f_pre_min = None
        for si in range(n_s):
            fpre_ref[si] = jnp.dot(src_ref[si].astype(BF16), w_in_ref[:, f_cols],
                                   preferred_element_type=F32)
            m = jnp.min(fpre_ref[si], axis=0, keepdims=True)
            f_pre_min = m if f_pre_min is None else jnp.minimum(f_pre_min, m)
            yield
        fgt_min = jnp.min(lb + (1.0 - lb) * jax.nn.sigmoid(f_pre_min))
        flag_ref[0] = (fgt_min >= MIN_FORGET_LOCAL).astype(jnp.int32)

    @pl.when((pl.program_id(0) == 0) & (ti == 0))
    def _():
        _interleave([project_gates(x_ref)])

    local_ok = flag_ref[0] == 1

    def stream(si, fast):
        x = x_ref[si]
        xb = x.astype(BF16)
        left = jnp.dot(xb, w_in_ref[:, 0:pw + hw], preferred_element_type=F32)
        right = jnp.dot(xb, w_in_ref[:, pw + 2 * hw:pw + 4 * hw], preferred_element_type=F32)
        xp = left[:, 0:pw]
        q = left[:, pw:pw + hw]
        f_pre = fpre_ref[si]
        v = right[:, 0:hw]
        g = right[:, hw:2 * hw]
        yield

        ext = jnp.concatenate([halo_ref[si], xp], axis=0)
        halo_ref[si] = xp[tm - POOL_HALO:, :]
        ds = []
        for gi, w in enumerate(POOL_WINDOWS):
            cols = slice(gi * GROUP, (gi + 1) * GROUP)
            s = ext[:, cols]
            width = 1
            while width < w:
                s = s + pltpu.roll(s, width, 0)
                width *= 2
            ds.append((s[POOL_HALO:, :] * inv_cnt[gi] - xp[:, cols]).astype(BF16))
        fgt = lb + (1.0 - lb) * jax.nn.sigmoid(f_pre)
        lf = jnp.log(fgt)
        kk = 1.0 - fgt
        qf = q * jax.nn.sigmoid(q)
        b = _chunk_cumsum(lf)
        yield

        y_a = jnp.concatenate(
            [jnp.dot(ds[gi], pool_w_ref[gi], preferred_element_type=F32)
             for gi in range(len(POOL_WINDOWS))], axis=-1) * pool_scale_ref[...]
        yield

        if fast:
            z = _level_operands(b, fgt, qf, kk, min_half=LOCAL)
            q_loc, k_loc = _local_operands(b, qf, kk)
        else:
            z = _level_operands(b, fgt, qf, kk)
            terms = [(zl, zl, masks[li]) for li, zl in enumerate(z)]
            qk = qf * kk
        qd = (qf * jnp.exp(b)).astype(BF16)
        vb = v.astype(BF16)
        kd, dec = [], []
        for c in range(n_chunks):
            b_c = b[c * CHUNK:(c + 1) * CHUNK, :]
            b_last = b_c[CHUNK - 1:CHUNK, :]
            kd.append((kk[c * CHUNK:(c + 1) * CHUNK, :] * jnp.exp(b_last - b_c)).astype(BF16))
            dec.append(jnp.exp(b_last))
        yield

        p_lv = {}
        upd = {}
        for c in range(n_chunks):
            rows = slice(c * CHUNK, (c + 1) * CHUNK)
            for hd in range(heads):
                cols = slice(hd * GROUP, (hd + 1) * GROUP)
                if fast:
                    lhs, rhs = _slot_operands([zl[rows, cols] for zl in z], LEVELS[:len(z)])
                    pairs = [(lhs, rhs), (q_loc[rows, cols], k_loc[rows, cols])]
                else:
                    pairs = [(lhs[rows, cols], rhs[rows, cols]) for lhs, rhs, _ in terms]
                p_lv[c, hd] = [lax.dot_general(lhs, rhs, NT_DIMS, preferred_element_type=F32)
                               for lhs, rhs in pairs]
                upd[c, hd] = lax.dot_general(vb[rows, cols], kd[c][:, cols], TN_DIMS,
                                             preferred_element_type=F32)
            yield

        o_intra = {}
        for c in range(n_chunks):
            rows = slice(c * CHUNK, (c + 1) * CHUNK)
            for hd in range(heads):
                cols = slice(hd * GROUP, (hd + 1) * GROUP)
                if fast:
                    a = jnp.where(local_mask, p_lv[c, hd][1], p_lv[c, hd][0])
                    o_intra[c, hd] = (a.astype(BF16), v[rows, cols].T.astype(BF16))
                else:
                    a = jnp.zeros((CHUNK, CHUNK), F32)
                    for (_, _, mask), p in zip(terms, p_lv[c, hd]):
                        a = jnp.where(mask, p, a)
                    o_intra[c, hd] = jnp.dot(a.astype(BF16), vb[rows, cols],
                                             preferred_element_type=F32)
            yield

        o_inter = {}
        for hd in range(heads):
            cols = slice(hd * GROUP, (hd + 1) * GROUP)
            st = st_ref[si, hd]
            for c in range(n_chunks):
                rows = slice(c * CHUNK, (c + 1) * CHUNK)
                if fast:
                    a_bf, vt_bf = o_intra[c, hd]
                    o_inter[c, hd] = lax.dot_general(
                        jnp.concatenate([qd[rows, cols], a_bf], axis=1),
                        jnp.concatenate([st.astype(BF16), vt_bf], axis=1), NT_DIMS,
                        preferred_element_type=F32)
                else:
                    o_inter[c, hd] = o_intra[c, hd] + lax.dot_general(
                        qd[rows, cols], st.astype(BF16), NT_DIMS, preferred_element_type=F32)
                st = dec[c][:, cols] * st + upd[c, hd]
            st_ref[si, hd] = st
        yield

        o_heads = []
        for hd in range(heads):
            cols = slice(hd * GROUP, (hd + 1) * GROUP)
            o_h = jnp.concatenate([o_inter[c, hd] for c in range(n_chunks)], axis=0)
            if not fast:
                diag = jnp.sum(qk[:, cols], axis=-1, keepdims=True)
                o_h = o_h + diag * v[:, cols]
            ms = jnp.mean(o_h * o_h, axis=-1, keepdims=True)
            g_h = g[:, cols]
            o_heads.append(o_h * lax.rsqrt(ms + RMS_EPS) * gn * (g_h * jax.nn.sigmoid(g_h)))
        y = jnp.concatenate([y_a] + o_heads, axis=-1).astype(BF16)
        yield

        mix = jnp.dot(y, w_out_ref[...], preferred_element_type=F32)
        yield
        out_ref[si] = _layer_norm(alpha * x + mix, g1_ref[...], b1_ref[...])

    def body(fast):
        _interleave([stream(si, fast) for si in range(n_s)],
                    fill=[project_gates(xn_ref)], plan=GATES_AHEAD_PLAN)

    pl.when(local_ok)(functools.partial(body, True))
    pl.when(jnp.logical_not(local_ok))(functools.partial(body, False))


def _gelu_tanh(a):
    c = 0.7978845608028654
    return 0.5 * a * (1.0 + jnp.tanh(c * (a + 0.044715 * (a * a * a))))


def _ffn_kernel(h_ref, w_up_ref, conv_w_ref, conv_b_ref, w_down_ref, g2_ref, b2_ref,
                out_ref, halo_ref, pin_ref, pout_ref, gated_ref, *, tm, alpha, d_ff):
    ti = pl.program_id(1)
    r = tm // SUBLANES
    n_cb = h_ref.shape[2] // GROUP
    n_chunks = d_ff // FF_CHUNK
    chunks = [(slice(j * FF_CHUNK, (j + 1) * FF_CHUNK),
               slice(d_ff + j * FF_CHUNK, d_ff + (j + 1) * FF_CHUNK)) for j in range(n_chunks)]

    @pl.when(ti == 0)
    def _():
        halo_ref[...] = jnp.zeros_like(halo_ref)

    def stream(si):
        for c in range(n_cb):
            pin_ref[si, c] = h_ref[si, :, c * GROUP:(c + 1) * GROUP]
        h = jnp.concatenate(
            [jnp.concatenate([pin_ref[si, c, pl.ds(s, r, stride=SUBLANES), :]
                              for c in range(n_cb)], axis=1) for s in range(SUBLANES)], axis=0)
        hb = h.astype(BF16)

        def up(cols):
            return jnp.dot(hb, w_up_ref[:, cols], preferred_element_type=F32)

        def conv(u, cols):
            shifted = []
            for k in range(CONV_WIDTH - 1):
                blk = u[(6 + k) * r:(7 + k) * r, :]
                ext = jnp.concatenate([halo_ref[si, k, :, cols], blk], axis=0)
                halo_ref[si, k, :, cols] = blk[r - CONV_HALO:, :]
                shifted.append(pltpu.roll(ext, 1, 0)[CONV_HALO:, :])
            u1 = jnp.concatenate([shifted[1], u[:7 * r, :]], axis=0)
            u2 = jnp.concatenate(shifted + [u[:6 * r, :]], axis=0)
            cw = conv_w_ref[:, cols]
            return conv_b_ref[:, cols] + cw[0:1, :] * u2 + cw[1:2, :] * u1 + cw[2:3, :] * u

        for ca, cv in chunks:
            gated_ref[si, :, ca] = (_gelu_tanh(conv(up(ca), ca)) * conv(up(cv), cv)).astype(BF16)
        yield
        acc = jnp.dot(gated_ref[si], w_down_ref[...], preferred_element_type=F32)
        res = _layer_norm(alpha * h + acc, g2_ref[...], b2_ref[...])
        for c in range(n_cb):
            for s in range(SUBLANES):
                pout_ref[si, c, pl.ds(s, r, stride=SUBLANES), :] = (
                    res[s * r:(s + 1) * r, c * GROUP:(c + 1) * GROUP])
        for c in range(n_cb):
            out_ref[si, :, c * GROUP:(c + 1) * GROUP] = pout_ref[si, c]

    _interleave([stream(si) for si in range(h_ref.shape[0])])


def _const_spec(shape):
    nd = len(shape)
    return pl.BlockSpec(shape, lambda b, t: (0,) * nd, pipeline_mode=pl.Buffered(1))


def _nbytes(shape, dtype):
    return math.prod(shape) * jnp.dtype(dtype).itemsize


def _vmem_limit(consts, blocks, scratch):
    total = (sum(_nbytes(c.shape, c.dtype) for c in consts)
             + 2 * sum(_nbytes(*b) for b in blocks)
             + sum(_nbytes(s.shape, s.dtype) for s in scratch)
             + TEMPORARIES_VMEM_BYTES)
    assert total <= V7X_VMEM_BYTES, "tile plan does not fit VMEM"
    return total


def _mixer_ln1(x, w_in, pool_w, pool_scale, lb, gnorm, w_out, g1, b1, *, alpha):
    bsz, seq, d = x.shape
    tm = MIXER_TILE
    heads = lb.shape[1] // GROUP
    kernel = functools.partial(_mixer_kernel, tm=tm, alpha=alpha, heads=heads)
    consts = (w_in, pool_w, pool_scale, lb, gnorm, w_out, g1, b1)
    n_t = seq // tm
    n_tiles = (bsz // STREAMS) * n_t

    def next_tile(b, t):
        g = jnp.minimum(b * n_t + t + 1, n_tiles - 1)
        return (g // n_t, g % n_t, 0)

    block = (STREAMS, tm, d)
    scratch = [
        pltpu.VMEM((STREAMS, tm, lb.shape[1]), F32),
        pltpu.SMEM((1,), jnp.int32),
        pltpu.VMEM((STREAMS, heads, GROUP, GROUP), F32),
        pltpu.VMEM((STREAMS, POOL_HALO, GROUP * len(POOL_WINDOWS)), F32),
    ]
    return pl.pallas_call(
        kernel,
        grid=(bsz // STREAMS, n_t),
        in_specs=[pl.BlockSpec(block, lambda b, t: (b, t, 0)),
                  pl.BlockSpec(block, next_tile)]
        + [_const_spec(c.shape) for c in consts],
        out_specs=pl.BlockSpec(block, lambda b, t: (b, t, 0)),
        out_shape=jax.ShapeDtypeStruct(x.shape, F32),
        scratch_shapes=scratch,
        compiler_params=pltpu.CompilerParams(
            dimension_semantics=("arbitrary", "arbitrary"),
            vmem_limit_bytes=_vmem_limit(consts, [(block, F32)] * 3, scratch)),
        name="mixer_ln1",
    )(x, x, *consts)


def _ffn_ln2(h, w_up, conv_w, conv_b, w_down, g2, b2, *, alpha):
    bsz, seq, d = h.shape
    tm = FFN_TILE
    d_ff = w_down.shape[0]
    kernel = functools.partial(_ffn_kernel, tm=tm, alpha=alpha, d_ff=d_ff)
    consts = (w_up, conv_w, conv_b, w_down, g2, b2)
    ns = FFN_STREAMS
    block = (ns, tm, d)
    scratch = [
        pltpu.VMEM((ns, CONV_WIDTH - 1, CONV_HALO, 2 * d_ff), F32),
        pltpu.VMEM((ns, d // GROUP, tm, GROUP), F32),
        pltpu.VMEM((ns, d // GROUP, tm, GROUP), F32),
        pltpu.VMEM((ns, tm, d_ff), BF16),
    ]
    return pl.pallas_call(
        kernel,
        grid=(bsz // ns, seq // tm),
        in_specs=[pl.BlockSpec(block, lambda b, t: (b, t, 0))]
        + [_const_spec(c.shape) for c in consts],
        out_specs=pl.BlockSpec(block, lambda b, t: (b, t, 0)),
        out_shape=jax.ShapeDtypeStruct(h.shape, F32),
        scratch_shapes=scratch,
        compiler_params=pltpu.CompilerParams(
            dimension_semantics=("arbitrary", "arbitrary"),
            vmem_limit_bytes=_vmem_limit(consts, [(block, F32)] * 2, scratch)),
        name="ffn_ln2",
    )(h, *consts)


def kernel(x, w_in, pool_w, pool_scale, hgrn_lb, hgrn_gnorm, w_out, ln1_g, ln1_b,
           w_up, conv_w, conv_b, w_down, ln2_g, ln2_b):
    depth = w_in.shape[0]
    assert depth == 1 and hgrn_lb.shape[0] == 2, "single-layer block expected"
    assert w_down.shape[1] % FF_CHUNK == 0
    assert x.shape[0] % STREAMS == 0 and x.shape[0] % FFN_STREAMS == 0
    assert x.shape[1] % MIXER_TILE == 0 and x.shape[1] % FFN_TILE == 0
    assert MIXER_TILE % CHUNK == 0 and FFN_TILE % (SUBLANES * CONV_HALO) == 0
    alpha = (2.0 * depth) ** 0.25
    h = _mixer_ln1(
        x, w_in[0].astype(BF16), pool_w[0].astype(BF16), pool_scale, hgrn_lb,
        hgrn_gnorm, w_out[0].astype(BF16), ln1_g, ln1_b, alpha=alpha)
    return _ffn_ln2(
        h, w_up[0].astype(BF16), conv_w[0], conv_b, w_down[0].astype(BF16),
        ln2_g, ln2_b, alpha=alpha)
```
